```python
import jax, jax.numpy as jnp
from jax import lax
import numpy as np

D_MODEL = 2048
BATCH = 8
SEQ = 2048
DEPTH = 1
DEC_BATCH = 128
DEC_SEQ = 1
PAST_LEN = 2048
PAGE_SIZE = 128

N_META = 16
SB_HEADS = 16
SB_HEAD_DIM = 64
D_ATTN = SB_HEADS * SB_HEAD_DIM
SB_SCALE = SB_HEAD_DIM ** -0.5
SB_BIAS_INIT = -7.0
Q_BLOCK = 128
D_CONV = D_MODEL // 2
CONV_WIDTH = 31
PEER_HEADS = 8
PEER_N_KEYS = 128
PEER_N_EXPERTS = PEER_N_KEYS * PEER_N_KEYS
PEER_QUERY_DIM = 256
PEER_HALF = PEER_QUERY_DIM // 2
PEER_TOPK = 16
PEER_TOKEN_CHUNK = 128
IN_COLS = 3 * D_ATTN + 2 * D_CONV + 2 * D_MODEL
SPLITS = (D_ATTN, 2 * D_ATTN, 3 * D_ATTN, 3 * D_ATTN + 2 * D_CONV, 3 * D_ATTN + 2 * D_CONV + D_MODEL)
RMS_EPS = 1e-6
LN_EPS = 1e-5

kernel_name = "hybrid_stickbreak_conformer_peer_step"


def rmsnorm(x, g):
    xf = x.astype(jnp.float32)
    y = xf * lax.rsqrt(jnp.mean(xf * xf, axis=-1, keepdims=True) + RMS_EPS)
    return (y * g.astype(jnp.float32)).astype(x.dtype)


def layernorm(x, g, b):
    xf = x.astype(jnp.float32)
    mu = jnp.mean(xf, axis=-1, keepdims=True)
    var = jnp.mean(jnp.square(xf - mu), axis=-1, keepdims=True)
    y = (xf - mu) * lax.rsqrt(var + LN_EPS)
    return (y * g.astype(jnp.float32) + b.astype(jnp.float32)).astype(x.dtype)


def split_projection(h, w_in):
    bsz, t = h.shape[0], h.shape[1]
    proj = jnp.einsum('btd,dc->btc', h, w_in)
    q, k, v, glu_in, gate_a, gate_c = jnp.split(proj, SPLITS, axis=-1)
    conv_in = glu_in[..., :D_CONV] * jax.nn.sigmoid(glu_in[..., D_CONV:])
    def heads(z):
        return z.reshape(bsz, t, SB_HEADS, SB_HEAD_DIM)
    return heads(q), heads(k), heads(v), conv_in, gate_a, gate_c


def stick_breaking(q, k, v, bias, q_pos, k_pos):
    z = jnp.einsum('bqhd,bshd->bhqs', q, k).astype(jnp.float32) * SB_SCALE
    z = z + bias.astype(jnp.float32)[None, :, None, None]
    visible = (k_pos[None, :] < q_pos[:, None])[None, None]
    log_1m = jnp.where(visible, jax.nn.log_sigmoid(-z), 0.0)
    tail = lax.cumsum(log_1m, axis=3, reverse=True) - log_1m
    a = jnp.where(visible, jnp.exp(jax.nn.log_sigmoid(z) + tail), 0.0)
    o = jnp.einsum('bhqs,bshd->bqhd', a.astype(v.dtype), v)
    return o.reshape(o.shape[0], o.shape[1], D_ATTN)


def sb_prompt(q, k, v, bias):
    bsz, t = q.shape[0], q.shape[1]
    n_blocks = (t - N_META) // Q_BLOCK
    key_pos = jnp.arange(t, dtype=jnp.int32)
    meta_pos = jnp.arange(N_META, dtype=jnp.int32)
    meta_out = stick_breaking(q[:, :N_META], k[:, :N_META], v[:, :N_META], bias, meta_pos, meta_pos)

    def block(b):
        start = N_META + b * Q_BLOCK
        qb = lax.dynamic_slice_in_dim(q, start, Q_BLOCK, axis=1)
        return stick_breaking(qb, k, v, bias, start + jnp.arange(Q_BLOCK, dtype=jnp.int32), key_pos)

    blocks = lax.map(block, jnp.arange(n_blocks, dtype=jnp.int32))
    real = jnp.moveaxis(blocks, 0, 1).reshape(bsz, n_blocks * Q_BLOCK, D_ATTN)
    return jnp.concatenate([meta_out, real], axis=1)


def conv_branch(padded, w_dw, b_dw, ln_g, ln_b):
    y = lax.conv_general_dilated(padded, w_dw[:, None, :], window_strides=(1,), padding='VALID',
                                 dimension_numbers=('NWC', 'WIO', 'NWC'), feature_group_count=D_CONV)
    y = y + b_dw
    return jax.nn.silu(layernorm(y, ln_g, ln_b))


def peer_ffn(h, w_query, sub_keys, expert_u, expert_v):
    shape = h.shape
    flat = h.reshape(-1, shape[-1])
    n = flat.shape[0]
    n_chunks = -(-n // PEER_TOKEN_CHUNK)
    flat = jnp.pad(flat, ((0, n_chunks * PEER_TOKEN_CHUNK - n), (0, 0)))

    def chunk(hc):
        qry = (hc @ w_query).reshape(-1, PEER_HEADS, 2, PEER_HALF)
        s = jnp.einsum('chpk,pnk->chpn', qry, sub_keys).astype(jnp.float32)
        s1, i1 = lax.top_k(s[:, :, 0], PEER_TOPK)
        s2, i2 = lax.top_k(s[:, :, 1], PEER_TOPK)
        cand_s = (s1[..., :, None] + s2[..., None, :]).reshape(-1, PEER_HEADS, PEER_TOPK * PEER_TOPK)
        cand_i = (i1[..., :, None] * PEER_N_KEYS + i2[..., None, :]).reshape(-1, PEER_HEADS, PEER_TOPK * PEER_TOPK)
        top_s, pos = lax.top_k(cand_s, PEER_TOPK)
        idx = jnp.take_along_axis(cand_i, pos, axis=-1)
        gate = jax.nn.softmax(top_s, axis=-1).astype(hc.dtype)
        act = jax.nn.gelu(jnp.einsum('cd,chkd->chk', hc, expert_u[idx]))
        return jnp.einsum('chk,chkd->cd', gate * act, expert_v[idx])

    out = lax.map(chunk, flat.reshape(n_chunks, PEER_TOKEN_CHUNK, shape[-1]))
    return out.reshape(-1, shape[-1])[:n].reshape(shape)


def layer_tail(x, attn, conv_act, gate_a, gate_c, w_pa, w_pc, w_o, g_ffn, w_query, sub_keys, expert_u, expert_v):
    branch_a = jnp.einsum('bta,ad->btd', attn, w_pa)
    branch_c = jnp.einsum('btc,cd->btd', conv_act, w_pc)
    mixed = jax.nn.sigmoid(gate_a) * branch_a + jax.nn.sigmoid(gate_c) * branch_c
    x = x + jnp.einsum('btd,de->bte', mixed, w_o)
    return x + peer_ffn(rmsnorm(x, g_ffn), w_query, sub_keys, expert_u, expert_v)


def setup_inputs(seed: int = 0) -> dict:
    key = jax.random.key(seed)
    ks = jax.random.split(key, 24)
    n_pages = PAST_LEN // PAGE_SIZE
    n_used = DEC_BATCH * n_pages
    n_pool = n_used + (n_used + 3) // 4
    f32 = jnp.float32
    def nrm(k, shape, scale):
        return jax.random.normal(k, shape, f32) * scale
    page_table = jax.random.permutation(ks[0], n_pool)[:n_used].reshape(DEC_BATCH, n_pages).astype(jnp.int32)
    return {
        'x_prompt': nrm(ks[1], (BATCH, SEQ, D_MODEL), 1.0),
        'x_sample': nrm(ks[2], (DEC_BATCH, DEC_SEQ, D_MODEL), 1.0),
        'cache_k': nrm(ks[3], (DEPTH, n_pool, PAGE_SIZE, SB_HEADS, SB_HEAD_DIM), 1.0),
        'cache_v': nrm(ks[4], (DEPTH, n_pool, PAGE_SIZE, SB_HEADS, SB_HEAD_DIM), 1.0),
        'state_conv': nrm(ks[5], (DEPTH, DEC_BATCH, CONV_WIDTH - 1, D_CONV), 0.5),
        'page_table': page_table,
        'meta_tokens': nrm(ks[6], (N_META, D_MODEL), 1.0),
        'norm_mix_g': 1.0 + nrm(ks[7], (DEPTH, D_MODEL), 0.02),
        'w_in': nrm(ks[8], (DEPTH, D_MODEL, IN_COLS), D_MODEL ** -0.5),
        'sb_bias': SB_BIAS_INIT + nrm(ks[22], (DEPTH, SB_HEADS), 0.1),
        'w_dw': nrm(ks[9], (DEPTH, CONV_WIDTH, D_CONV), CONV_WIDTH ** -0.5),
        'b_dw': nrm(ks[10], (DEPTH, D_CONV), 0.02),
        'conv_ln_g': 1.0 + nrm(ks[11], (DEPTH, D_CONV), 0.02),
        'conv_ln_b': nrm(ks[12], (DEPTH, D_CONV), 0.02),
        'w_branch_attn': nrm(ks[13], (DEPTH, D_ATTN, D_MODEL), D_ATTN ** -0.5),
        'w_branch_conv': nrm(ks[14], (DEPTH, D_CONV, D_MODEL), D_CONV ** -0.5),
        'w_out': nrm(ks[15], (DEPTH, D_MODEL, D_MODEL), D_MODEL ** -0.5),
        'norm_ffn_g': 1.0 + nrm(ks[16], (DEPTH, D_MODEL), 0.02),
        'w_query': nrm(ks[17], (DEPTH, D_MODEL, PEER_HEADS * PEER_QUERY_DIM), D_MODEL ** -0.5),
        'sub_keys': nrm(ks[18], (DEPTH, 2, PEER_N_KEYS, PEER_HALF), PEER_HALF ** -0.5),
        'expert_u': nrm(ks[19], (DEPTH, PEER_N_EXPERTS, D_MODEL), D_MODEL ** -0.5),
        'expert_v': nrm(ks[20], (DEPTH, PEER_N_EXPERTS, D_MODEL), 0.5 * PEER_HEADS ** -0.5),
        'norm_final_g': 1.0 + nrm(ks[21], (D_MODEL,), 0.02),
    }


def reference(x_prompt, x_sample, cache_k, cache_v, state_conv, page_table, meta_tokens, norm_mix_g, w_in,
              sb_bias, w_dw, b_dw, conv_ln_g, conv_ln_b, w_branch_attn, w_branch_conv, w_out, norm_ffn_g, w_query,
              sub_keys, expert_u, expert_v, norm_final_g):
    bsz = x_prompt.shape[0]
    dec_b, dec_s = x_sample.shape[0], x_sample.shape[1]
    past_len = page_table.shape[1] * cache_k.shape[2]
    meta = jnp.broadcast_to(meta_tokens[None].astype(x_prompt.dtype), (bsz, N_META, D_MODEL))
    xp = jnp.concatenate([meta, x_prompt], axis=1)
    xs = x_sample
    q_pos_s = past_len + jnp.arange(dec_s, dtype=jnp.int32)
    k_pos_s = jnp.arange(past_len + dec_s, dtype=jnp.int32)
    k_p_list, v_p_list, c_p_list, k_s_list, v_s_list, c_s_list = [], [], [], [], [], []
    for l in range(DEPTH):
        h = rmsnorm(xp, norm_mix_g[l])
        q, k, v, conv_in, gate_a, gate_c = split_projection(h, w_in[l])
        attn = sb_prompt(q, k, v, sb_bias[l])
        padded = jnp.concatenate([jnp.zeros((bsz, CONV_WIDTH - 1, D_CONV), conv_in.dtype), conv_in], axis=1)
        conv_act = conv_branch(padded, w_dw[l], b_dw[l], conv_ln_g[l], conv_ln_b[l])
        xp = layer_tail(xp, attn, conv_act, gate_a, gate_c, w_branch_attn[l], w_branch_conv[l], w_out[l],
                        norm_ffn_g[l], w_query[l], sub_keys[l], expert_u[l], expert_v[l])
        k_p_list.append(k)
        v_p_list.append(v)
        c_p_list.append(padded[:, -(CONV_WIDTH - 1):])
        hs = rmsnorm(xs, norm_mix_g[l])
        q_s, k_s, v_s, conv_in_s, gate_a_s, gate_c_s = split_projection(hs, w_in[l])
        k_past = cache_k[l][page_table].reshape(dec_b, past_len, SB_HEADS, SB_HEAD_DIM)
        v_past = cache_v[l][page_table].reshape(dec_b, past_len, SB_HEADS, SB_HEAD_DIM)
        k_all = jnp.concatenate([k_past, k_s], axis=1)
        v_all = jnp.concatenate([v_past, v_s], axis=1)
        attn_s = stick_breaking(q_s, k_all, v_all, sb_bias[l], q_pos_s, k_pos_s)
        padded_s = jnp.concatenate([state_conv[l].astype(conv_in_s.dtype), conv_in_s], axis=1)
        conv_act_s = conv_branch(padded_s, w_dw[l], b_dw[l], conv_ln_g[l], conv_ln_b[l])
        xs = layer_tail(xs, attn_s, conv_act_s, gate_a_s, gate_c_s, w_branch_attn[l], w_branch_conv[l], w_out[l],
                        norm_ffn_g[l], w_query[l], sub_keys[l], expert_u[l], expert_v[l])
        k_s_list.append(k_s)
        v_s_list.append(v_s)
        c_s_list.append(padded_s[:, -(CONV_WIDTH - 1):])
    y_prompt = rmsnorm(xp, norm_final_g)[:, N_META:]
    y_sample = rmsnorm(xs, norm_final_g)
    k_prompt = jnp.stack(k_p_list)
    v_prompt = jnp.stack(v_p_list)
    conv_prompt = jnp.stack(c_p_list)
    k_sample = jnp.stack(k_s_list)
    v_sample = jnp.stack(v_s_list)
    conv_sample = jnp.stack(c_s_list)
    return (y_prompt, y_sample, k_prompt, v_prompt, conv_prompt, k_sample, v_sample, conv_sample)
```

```python
import functools

import jax
import jax.numpy as jnp
from jax import lax
from jax.experimental import pallas as pl
from jax.experimental.pallas import tpu as pltpu

F32 = jnp.float32
BF16 = jnp.bfloat16

RMS_EPS = 1e-6
LN_EPS = 1e-5
PEER_TOPK = 16
LANES = 128
KEY_BLOCK = 128
CONV_HALO = 32
VMEM_LIMIT = 56 * 1024 * 1024


def _params(*sem):
    return pltpu.CompilerParams(dimension_semantics=sem, vmem_limit_bytes=VMEM_LIMIT)


def _tile(n, want):
    if n <= want:
        return n
    t = want
    while n % t:
        t //= 2
    return t


_NT = (((1,), (1,)), ((), ()))
_TN = (((0,), (0,)), ((), ()))


def _rmsnorm_kernel(x_ref, g_ref, o_ref):
    x = x_ref[...]
    ms = jnp.mean(x * x, axis=-1, keepdims=True)
    o_ref[...] = (x * lax.rsqrt(ms + RMS_EPS) * g_ref[...]).astype(o_ref.dtype)


def _rmsnorm(x, g, out_dtype):
    m, d = x.shape
    tm = _tile(m, 512)
    return pl.pallas_call(
        _rmsnorm_kernel,
        grid=(m // tm,),
        in_specs=[pl.BlockSpec((tm, d), lambda i: (i, 0)), pl.BlockSpec((1, d), lambda i: (0, 0))],
        out_specs=pl.BlockSpec((tm, d), lambda i: (i, 0)),
        out_shape=jax.ShapeDtypeStruct((m, d), out_dtype),
        compiler_params=_params("parallel"),
    )(x, g.reshape(1, d))


def _qkv_kernel(h_ref, w_ref, q_ref, k_ref, v_ref, kb_ref, vb_ref, *, scale):
    j = pl.program_id(1)
    r = jnp.dot(h_ref[...], w_ref[...], preferred_element_type=F32)

    @pl.when(j == 0)
    def _():
        q_ref[...] = (r * scale).astype(BF16)

    @pl.when(j == 1)
    def _():
        k_ref[...] = r
        kb_ref[...] = r.astype(BF16)

    @pl.when(j == 2)
    def _():
        v_ref[...] = r
        vb_ref[...] = r.astype(BF16)


def _qkv_proj(h, w_in, d_attn, scale):
    m, d = h.shape
    tm = _tile(m, 512)
    row = lambda i, j: (i, 0)
    out_spec = pl.BlockSpec((tm, d_attn), row)
    return pl.pallas_call(
        functools.partial(_qkv_kernel, scale=scale),
        grid=(m // tm, 3),
        in_specs=[pl.BlockSpec((tm, d), row), pl.BlockSpec((d, d_attn), lambda i, j: (0, j))],
        out_specs=[out_spec] * 5,
        out_shape=[jax.ShapeDtypeStruct((m, d_attn), t) for t in (BF16, F32, F32, BF16, BF16)],
        compiler_params=_params("parallel", "arbitrary"),
    )(h, w_in)


def _glu_kernel(h_ref, wa_ref, wb_ref, o_ref):
    h = h_ref[...]
    a = jnp.dot(h, wa_ref[...], preferred_element_type=F32)
    b = jnp.dot(h, wb_ref[...], preferred_element_type=F32)
    o_ref[...] = a * jax.nn.sigmoid(b)


def _glu_proj(h, w_in, col0, d_conv):
    m, d = h.shape
    tm = _tile(m, 512)
    tn = _tile(d_conv, 512)
    a0, b0 = col0 // tn, (col0 + d_conv) // tn
    return pl.pallas_call(
        _glu_kernel,
        grid=(m // tm, d_conv // tn),
        in_specs=[pl.BlockSpec((tm, d), lambda i, j: (i, 0)),
                  pl.BlockSpec((d, tn), lambda i, j: (0, a0 + j)),
                  pl.BlockSpec((d, tn), lambda i, j: (0, b0 + j))],
        out_specs=pl.BlockSpec((tm, tn), lambda i, j: (i, j)),
        out_shape=jax.ShapeDtypeStruct((m, d_conv), F32),
        compiler_params=_params("parallel", "arbitrary"),
    )(h, w_in, w_in)


def _gate_kernel(h_ref, w_ref, o_ref):
    o_ref[...] = jax.nn.sigmoid(jnp.dot(h_ref[...], w_ref[...], preferred_element_type=F32))


def _gate_proj(h, w_in, col0, n_cols):
    m, d = h.shape
    tm = _tile(m, 512)
    tn = _tile(n_cols, 1024)
    c0 = col0 // tn
    return pl.pallas_call(
        _gate_kernel,
        grid=(m // tm, n_cols // tn),
        in_specs=[pl.BlockSpec((tm, d), lambda i, j: (i, 0)),
                  pl.BlockSpec((d, tn), lambda i, j: (0, c0 + j))],
        out_specs=pl.BlockSpec((tm, tn), lambda i, j: (i, j)),
        out_shape=jax.ShapeDtypeStruct((m, n_cols), F32),
        compiler_params=_params("parallel", "arbitrary"),
    )(h, w_in)


def _softplus_parts(z):
    lp = jnp.log(1.0 + jnp.exp(-jnp.abs(z)))
    return lp, jnp.maximum(z, 0.0) + lp


def _split_bf16(x):
    hi = x.astype(BF16)
    return hi, (x - hi.astype(F32)).astype(BF16)


def _attn_prompt_kernel(bias_ref, q_ref, k_ref, v_ref, km_ref, vm_ref, u_ref, o_ref, acc_ref, carry_ref,
                        *, tq, n_meta, head_dim):
    i = pl.program_id(1)
    n_pairs = q_ref.shape[2] // LANES
    tk = KEY_BLOCK
    lane = lax.broadcasted_iota(jnp.int32, (1, LANES), 1)
    lane_sel = [lane < head_dim, lane >= head_dim]
    u_ext = u_ref[...]

    acc_ref[...] = jnp.zeros_like(acc_ref)
    carry_ref[...] = jnp.zeros_like(carry_ref)

    def sweep(k_at, v_at, mask):
        for hp in range(n_pairs):
            cols = slice(hp * LANES, (hp + 1) * LANES)
            qp = q_ref[0, :, cols]
            kp = k_at(cols)
            vp = v_at(cols)
            for hh in range(2):
                h = 2 * hp + hh
                qh = jnp.where(lane_sel[hh], qp, jnp.zeros_like(qp))
                vh = jnp.where(lane_sel[hh], vp, jnp.zeros_like(vp))
                z = lax.dot_general(qh, kp, _NT, preferred_element_type=F32) + bias_ref[h]
                lp, sp = _softplus_parts(z)
                if mask is not None:
                    sp = jnp.where(mask, sp, 0.0)
                hi, lo = _split_bf16(sp)
                t = (jnp.dot(hi, u_ext, preferred_element_type=F32)
                     + jnp.dot(lo, u_ext, preferred_element_type=F32))
                tail = t[:, :tk] + carry_ref[h]
                a = jnp.exp((jnp.minimum(z, 0.0) - lp) + tail)
                if mask is not None:
                    a = jnp.where(mask, a, 0.0)
                acc_ref[:, cols] += jnp.dot(a.astype(BF16), vh, preferred_element_type=F32)
                carry_ref[h] += t[:, tk:]

    def real_block(kb):
        off = pl.multiple_of(kb * tk, tk)
        return (lambda cols: k_ref[0, pl.ds(off, tk), cols]), (lambda cols: v_ref[0, pl.ds(off, tk), cols])

    rows = lax.broadcasted_iota(jnp.int32, (tq, tk), 0)
    keys = lax.broadcasted_iota(jnp.int32, (tq, tk), 1)
    n_diag = tq // tk
    for d in reversed(range(n_diag)):
        sweep(*real_block(i * n_diag + d), keys + d * tk < rows)

    def body(n, c):
        sweep(*real_block(i * n_diag - 1 - n), None)
        return c

    lax.fori_loop(0, i * n_diag, body, 0)
    sweep(lambda cols: km_ref[:, cols], lambda cols: vm_ref[:, cols], keys < n_meta)
    o_ref[0] = acc_ref[...].astype(o_ref.dtype)


def _attn_prompt(q, kb, vb, k_meta, v_meta, bias, head_dim):
    bsz, t, da = q.shape
    n_meta = k_meta.shape[0]
    assert 2 * head_dim == LANES and n_meta <= KEY_BLOCK and t % KEY_BLOCK == 0
    tq = KEY_BLOCK
    pad = ((0, KEY_BLOCK - n_meta), (0, 0))
    km = jnp.pad(k_meta, pad)
    vm = jnp.pad(v_meta, pad)
    jj = lax.broadcasted_iota(jnp.int32, (KEY_BLOCK, KEY_BLOCK + LANES), 0)
    ss = lax.broadcasted_iota(jnp.int32, (KEY_BLOCK, KEY_BLOCK + LANES), 1)
    u_ext = jnp.where((ss >= KEY_BLOCK) | (jj > ss), -1.0, 0.0).astype(BF16)
    n_heads = da // head_dim
    const = lambda b, i: (0, 0)
    return pl.pallas_call(
        functools.partial(_attn_prompt_kernel, tq=tq, n_meta=n_meta, head_dim=head_dim),
        grid=(bsz, t // tq),
        in_specs=[pl.BlockSpec(memory_space=pltpu.SMEM),
                  pl.BlockSpec((1, tq, da), lambda b, i: (b, i, 0)),
                  pl.BlockSpec((1, t, da), lambda b, i: (b, 0, 0)),
                  pl.BlockSpec((1, t, da), lambda b, i: (b, 0, 0)),
                  pl.BlockSpec((KEY_BLOCK, da), const),
                  pl.BlockSpec((KEY_BLOCK, da), const),
                  pl.BlockSpec((KEY_BLOCK, KEY_BLOCK + LANES), const)],
        out_specs=pl.BlockSpec((1, tq, da), lambda b, i: (b, i, 0)),
        out_shape=jax.ShapeDtypeStruct((bsz, t, da), BF16),
        scratch_shapes=[pltpu.VMEM((tq, da), F32), pltpu.VMEM((n_heads, tq, LANES), F32)],
        compiler_params=_params("parallel", "arbitrary"),
    )(bias, q, kb, vb, km, vm, u_ext)


def _attn_decode_kernel(pt_ref, q_ref, bias_ref, e_ref, et_ref, ut_ref, *refs, n_pages):
    del pt_ref
    k_refs, v_refs, o_ref = refs[:n_pages], refs[n_pages:2 * n_pages], refs[2 * n_pages]
    q = q_ref[0].astype(F32)
    bias = bias_ref[...]
    ut = ut_ref[...]
    carry = jnp.zeros((1, LANES), F32)
    o = jnp.zeros(q.shape, F32)
    for p in reversed(range(n_pages)):
        prod = (k_refs[p][0] * q).astype(BF16)
        z = jnp.dot(prod, e_ref[...], preferred_element_type=F32) + bias
        lp, sp = _softplus_parts(z)
        hi, lo = _split_bf16(sp)
        tail = (jnp.dot(ut, hi, preferred_element_type=F32)
                + jnp.dot(ut, lo, preferred_element_type=F32)) + carry
        a = jnp.exp((jnp.minimum(z, 0.0) - lp) + tail)
        carry = carry - jnp.sum(sp, axis=0, keepdims=True)
        a_wide = jnp.dot(a.astype(BF16), et_ref[...], preferred_element_type=F32)
        o = o + jnp.sum(a_wide * v_refs[p][0], axis=0, keepdims=True)
    o_ref[0] = o.astype(o_ref.dtype)


def _attn_decode(q, cache_k, cache_v, page_table, bias, head_dim):
    n, da = q.shape
    n_pages = page_table.shape[1]
    page = cache_k.shape[1]
    n_heads = da // head_dim
    assert n_heads <= LANES
    head_of = lax.broadcasted_iota(jnp.int32, (da, LANES), 0) // head_dim
    e = (head_of == lax.broadcasted_iota(jnp.int32, (da, LANES), 1)).astype(BF16)
    rr = lax.broadcasted_iota(jnp.int32, (page, page), 0)
    cc = lax.broadcasted_iota(jnp.int32, (page, page), 1)
    ut = jnp.where(cc > rr, -1.0, 0.0).astype(BF16)
    bias_row = jnp.pad(bias, (0, LANES - n_heads)).reshape(1, LANES)
    const = lambda s, pt: (0, 0)

    def page_spec(p):
        return pl.BlockSpec((1, page, da), lambda s, pt, p=p: (pt[s, p], 0, 0))

    grid_spec = pltpu.PrefetchScalarGridSpec(
        num_scalar_prefetch=1,
        grid=(n,),
        in_specs=[pl.BlockSpec((1, 1, da), lambda s, pt: (s, 0, 0)),
                  pl.BlockSpec((1, LANES), const),
                  pl.BlockSpec((da, LANES), const),
                  pl.BlockSpec((LANES, da), const),
                  pl.BlockSpec((page, page), const)]
                 + [page_spec(p) for p in range(n_pages)] * 2,
        out_specs=pl.BlockSpec((1, 1, da), lambda s, pt: (s, 0, 0)),
    )
    out = pl.pallas_call(
        functools.partial(_attn_decode_kernel, n_pages=n_pages),
        grid_spec=grid_spec,
        out_shape=jax.ShapeDtypeStruct((n, 1, da), BF16),
        compiler_params=_params("arbitrary"),
    )(page_table, q.reshape(n, 1, da), bias_row, e, e.T, ut, *([cache_k] * n_pages), *([cache_v] * n_pages))
    return out.reshape(n, da)


def _ln_swish(y, g, b):
    mu = jnp.mean(y, axis=-1, keepdims=True)
    var = jnp.mean(jnp.square(y - mu), axis=-1, keepdims=True)
    yn = (y - mu) * lax.rsqrt(var + LN_EPS) * g + b
    return yn * jax.nn.sigmoid(yn)


def _conv_prompt_kernel(cur_ref, halo_ref, first_ref, w_ref, b_ref, g_ref, beta_ref, o_ref, win_ref, y_ref,
                        *, width, row_chunk):
    i = pl.program_id(1)
    tt, dc = cur_ref.shape[1], cur_ref.shape[2]
    win_ref[:CONV_HALO] = jnp.where(i == 0, first_ref[...], halo_ref[0])
    win_ref[CONV_HALO:] = cur_ref[0]
    base = CONV_HALO - (width - 1)

    def lane_chunk(c, carry):
        cols = pl.ds(pl.multiple_of(c * LANES, LANES), LANES)
        for r in range(tt // row_chunk):
            acc = jnp.zeros((row_chunk, LANES), F32)
            for w in range(width):
                acc = acc + win_ref[pl.ds(r * row_chunk + base + w, row_chunk), cols] * w_ref[pl.ds(w, 1), cols]
            y_ref[pl.ds(r * row_chunk, row_chunk), cols] = acc
        return carry

    lax.fori_loop(0, dc // LANES, lane_chunk, 0)
    o_ref[0] = _ln_swish(y_ref[...] + b_ref[...], g_ref[...], beta_ref[...]).astype(o_ref.dtype)


def _conv_prompt(conv_in, conv_meta, w_dw, b_dw, ln_g, ln_b):
    bsz, t, dc = conv_in.shape
    width = w_dw.shape[0]
    assert width - 1 <= CONV_HALO and dc % LANES == 0
    tt = _tile(t, 128)
    assert tt % CONV_HALO == 0
    first = jnp.concatenate([jnp.zeros((CONV_HALO, dc), F32), conv_meta], axis=0)[-CONV_HALO:]
    per = tt // CONV_HALO
    vec = lambda a: a.reshape(1, dc)
    const = lambda b, i: (0, 0)
    return pl.pallas_call(
        functools.partial(_conv_prompt_kernel, width=width, row_chunk=_tile(tt, 64)),
        grid=(bsz, t // tt),
        in_specs=[pl.BlockSpec((1, tt, dc), lambda b, i: (b, i, 0)),
                  pl.BlockSpec((1, CONV_HALO, dc), lambda b, i: (b, jnp.maximum(i * per - 1, 0), 0)),
                  pl.BlockSpec((CONV_HALO, dc), const),
                  pl.BlockSpec((width, dc), const),
                  pl.BlockSpec((1, dc), const), pl.BlockSpec((1, dc), const), pl.BlockSpec((1, dc), const)],
        out_specs=pl.BlockSpec((1, tt, dc), lambda b, i: (b, i, 0)),
        out_shape=jax.ShapeDtypeStruct((bsz, t, dc), BF16),
        scratch_shapes=[pltpu.VMEM((CONV_HALO + tt, dc), F32), pltpu.VMEM((tt, dc), F32)],
        compiler_params=_params("parallel", "arbitrary"),
    )(conv_in, conv_in, first, w_dw, vec(b_dw), vec(ln_g), vec(ln_b))


def _conv_decode_kernel(state_ref, new_ref, w_ref, b_ref, g_ref, beta_ref, o_ref, *, width):
    y = jnp.sum(state_ref[...] * w_ref[pl.ds(0, width - 1), :][None], axis=1)
    y = y + new_ref[...] * w_ref[pl.ds(width - 1, 1), :] + b_ref[...]
    o_ref[...] = _ln_swish(y, g_ref[...], beta_ref[...]).astype(o_ref.dtype)


def _conv_decode(state, new, w_dw, b_dw, ln_g, ln_b):
    n, hist, dc = state.shape
    width = w_dw.shape[0]
    assert hist == width - 1
    tn = _tile(n, 8)
    vec = lambda a: a.reshape(1, dc)
    const = lambda i: (0, 0)
    return pl.pallas_call(
        functools.partial(_conv_decode_kernel, width=width),
        grid=(n // tn,),
        in_specs=[pl.BlockSpec((tn, hist, dc), lambda i: (i, 0, 0)),
                  pl.BlockSpec((tn, dc), lambda i: (i, 0)),
                  pl.BlockSpec((width, dc), const),
                  pl.BlockSpec((1, dc), const), pl.BlockSpec((1, dc), const), pl.BlockSpec((1, dc), const)],
        out_specs=pl.BlockSpec((tn, dc), lambda i: (i, 0)),
        out_shape=jax.ShapeDtypeStruct((n, dc), BF16),
        compiler_params=_params("parallel"),
    )(state, new, w_dw, vec(b_dw), vec(ln_g), vec(ln_b))


def _merge_kernel(attn_ref, cact_ref, ga_ref, gc_ref, x_ref, wpa_ref, wpc_ref, wo_ref, g_ref, x1_ref, h2_ref):
    br_a = jnp.dot(attn_ref[...], wpa_ref[...], preferred_element_type=F32)
    br_c = jnp.dot(cact_ref[...], wpc_ref[...], preferred_element_type=F32)
    mixed = (ga_ref[...] * br_a + gc_ref[...] * br_c).astype(BF16)
    x1 = x_ref[...] + jnp.dot(mixed, wo_ref[...], preferred_element_type=F32)
    x1_ref[...] = x1
    ms = jnp.mean(x1 * x1, axis=-1, keepdims=True)
    h2_ref[...] = (x1 * lax.rsqrt(ms + RMS_EPS) * g_ref[...]).astype(BF16)


def _merge(attn, cact, gates, x, w_pa, w_pc, w_o, g_ffn):
    m, d = x.shape
    da, dc = attn.shape[1], cact.shape[1]
    tm = _tile(m, 256)
    row = lambda i: (i, 0)
    const = lambda i: (0, 0)
    once = pl.Buffered(1)
    return pl.pallas_call(
        _merge_kernel,
        grid=(m // tm,),
        in_specs=[pl.BlockSpec((tm, da), row), pl.BlockSpec((tm, dc), row),
                  pl.BlockSpec((tm, d), row), pl.BlockSpec((tm, d), lambda i: (i, 1)),
                  pl.BlockSpec((tm, d), row),
                  pl.BlockSpec((da, d), const, pipeline_mode=once),
                  pl.BlockSpec((dc, d), const, pipeline_mode=once),
                  pl.BlockSpec((d, d), const, pipeline_mode=once),
                  pl.BlockSpec((1, d), const)],
        out_specs=[pl.BlockSpec((tm, d), row), pl.BlockSpec((tm, d), row)],
        out_shape=[jax.ShapeDtypeStruct((m, d), F32), jax.ShapeDtypeStruct((m, d), BF16)],
        compiler_params=_params("parallel"),
    )(attn, cact, gates, gates, x, w_pa, w_pc, w_o, g_ffn.reshape(1, d))


def _top_desc(s, top_ref, k):
    def body(r, s):
        m = jnp.max(s, axis=0, keepdims=True)
        top_ref[pl.ds(r, 1), :] = m
        return jnp.where(s == m, -jnp.inf, s)

    lax.fori_loop(0, k, body, s)


def _route_kernel(h_ref, wq_ref, sk_ref, t1_ref, s2_ref, e1_ref, e2_ref, qry_ref, ta_ref, tb_ref, tc_ref,
                  *, n_heads, half, topk):
    qry_ref[...] = jnp.dot(h_ref[...], wq_ref[...], preferred_element_type=F32).astype(BF16)
    for h in range(n_heads):
        s = []
        for p in range(2):
            col = (2 * h + p) * half
            s.append(lax.dot_general(sk_ref[p], qry_ref[:, col:col + half], _NT, preferred_element_type=F32))
        _top_desc(s[0], ta_ref, topk)
        _top_desc(s[1], tb_ref, topk)
        a = ta_ref[...]
        b = tb_ref[...]
        cand = jnp.concatenate([a[p:p + 1] + b for p in range(topk)], axis=0)
        _top_desc(cand, tc_ref, topk)
        c = tc_ref[...]
        zsum = jnp.sum(jnp.exp(c - c[0:1]), axis=0, keepdims=True)
        t1 = c[topk - 1:topk] - s[0]
        e1 = jnp.exp(s[0] - a[0:1]) / zsum
        by_key = pl.ds(h, t1.shape[0], stride=n_heads)
        for lt in range(t1.shape[1] // LANES):
            cols = slice(lt * LANES, (lt + 1) * LANES)
            t1_ref[lt, by_key, :] = t1[:, cols]
            e1_ref[lt, by_key, :] = e1[:, cols]
        s2_ref[h] = s[1]
        e2_ref[h] = jnp.exp(s[1] - b[0:1])


def _route(h2, w_query, sub_keys):
    m, d = h2.shape
    n_keys, half = sub_keys.shape[1], sub_keys.shape[2]
    qd = w_query.shape[1]
    n_heads = qd // (2 * half)
    tm = _tile(m, 256)
    assert tm % LANES == 0
    by_head = pl.BlockSpec((n_heads, n_keys, tm), lambda i: (0, 0, i))
    by_key = pl.BlockSpec((tm // LANES, n_keys * n_heads, LANES), lambda i: (i, 0, 0))
    shape_by_head = jax.ShapeDtypeStruct((n_heads, n_keys, m), F32)
    shape_by_key = jax.ShapeDtypeStruct((m // LANES, n_keys * n_heads, LANES), F32)
    return pl.pallas_call(
        functools.partial(_route_kernel, n_heads=n_heads, half=half, topk=PEER_TOPK),
        grid=(m // tm,),
        in_specs=[pl.BlockSpec((tm, d), lambda i: (i, 0)),
                  pl.BlockSpec((d, qd), lambda i: (0, 0)),
                  pl.BlockSpec((2, n_keys, half), lambda i: (0, 0, 0))],
        out_specs=[by_key, by_head, by_key, by_head],
        out_shape=[shape_by_key, shape_by_head, shape_by_key, shape_by_head],
        scratch_shapes=[pltpu.VMEM((tm, qd), BF16)] + [pltpu.VMEM((PEER_TOPK, tm), F32)] * 3,
        compiler_params=_params("parallel"),
    )(h2, w_query, sub_keys)


def _peer_kernel(h_ref, u_ref, v_ref, t1_ref, s2_ref, e1_ref, e2_ref, x1_ref, g_ref, y_ref,
                 hu_ref, w_ref, acc_ref, *, n_keys):
    e = pl.program_id(1)
    te, tc = hu_ref.shape
    n_heads = s2_ref.shape[0]
    n_i = te // n_keys
    n_lt = tc // LANES

    @pl.when(e == 0)
    def _():
        acc_ref[...] = jnp.zeros_like(acc_ref)

    hu_ref[...] = lax.dot_general(u_ref[...], h_ref[...], _NT, preferred_element_type=F32)

    def body(n, carry):
        il = n // n_lt
        lt = n - il * n_lt
        i = e * n_i + il
        lanes = pl.ds(pl.multiple_of(lt * LANES, LANES), LANES)
        rows = pl.ds(pl.multiple_of(il * n_keys, n_keys), n_keys)
        heads = pl.ds(pl.multiple_of(i * n_heads, n_heads), n_heads)
        t1 = t1_ref[lt, heads, :]
        e1 = e1_ref[lt, heads, :]
        g = jnp.zeros((n_keys, LANES), F32)
        for h in range(n_heads):
            hit = s2_ref[h, :, lanes] >= t1[h:h + 1]
            g = g + jnp.where(hit, e1[h:h + 1] * e2_ref[h, :, lanes], 0.0)
        w_ref[rows, lanes] = (jax.nn.gelu(hu_ref[rows, lanes]) * g).astype(BF16)
        return carry

    lax.fori_loop(0, n_i * n_lt, body, 0)
    acc_ref[...] += lax.dot_general(w_ref[...], v_ref[...], _TN, preferred_element_type=F32)

    @pl.when(e == pl.num_programs(1) - 1)
    def _():
        x2 = x1_ref[...] + acc_ref[...]
        ms = jnp.mean(x2 * x2, axis=-1, keepdims=True)
        y_ref[...] = x2 * lax.rsqrt(ms + RMS_EPS) * g_ref[...]


def _peer(h2, tables, x1, expert_u, expert_v, g_final):
    m, d = h2.shape
    n_exp = expert_u.shape[0]
    n_heads, n_keys = tables[1].shape[0], tables[1].shape[1]
    tc = _tile(m, 512)
    te = _tile(n_exp, 512)
    assert te % n_keys == 0 and tc % LANES == 0 and n_heads % 8 == 0
    tok = lambda c, e: (c, 0)
    by_head = pl.BlockSpec((n_heads, n_keys, tc), lambda c, e: (0, 0, c))
    by_key = pl.BlockSpec((tc // LANES, n_keys * n_heads, LANES), lambda c, e: (c, 0, 0))
    return pl.pallas_call(
        functools.partial(_peer_kernel, n_keys=n_keys),
        grid=(m // tc, n_exp // te),
        in_specs=[pl.BlockSpec((tc, d), tok),
                  pl.BlockSpec((te, d), lambda c, e: (e, 0)),
                  pl.BlockSpec((te, d), lambda c, e: (e, 0)),
                  by_key, by_head, by_key, by_head,
                  pl.BlockSpec((tc, d), tok),
                  pl.BlockSpec((1, d), lambda c, e: (0, 0))],
        out_specs=pl.BlockSpec((tc, d), tok),
        out_shape=jax.ShapeDtypeStruct((m, d), F32),
        scratch_shapes=[pltpu.VMEM((te, tc), F32), pltpu.VMEM((te, tc), BF16), pltpu.VMEM((tc, d), F32)],
        compiler_params=_params("parallel", "arbitrary"),
    )(h2, expert_u, expert_v, *tables, x1, g_final.reshape(1, d))


def _token_mixer_tail(x, attn, cact, gates, w_pa, w_pc, w_o, g_ffn, w_query, sub_keys, expert_u, expert_v, g_final):
    x1, h2 = _merge(attn, cact, gates, x, w_pa, w_pc, w_o, g_ffn)
    tables = _route(h2, w_query, sub_keys)
    return _peer(h2, tables, x1, expert_u, expert_v, g_final)


def kernel(x_prompt, x_sample, cache_k, cache_v, state_conv, page_table, meta_tokens, norm_mix_g, w_in, sb_bias,
           w_dw, b_dw, conv_ln_g, conv_ln_b, w_branch_attn, w_branch_conv, w_out, norm_ffn_g, w_query, sub_keys,
           expert_u, expert_v, norm_final_g):
    depth = w_in.shape[0]
    assert depth == 1, "the meta-token shortcut below holds for a single layer"
    bsz, t, d = x_prompt.shape
    n_dec = x_sample.shape[0]
    assert x_sample.shape[1] == 1
    n_meta = meta_tokens.shape[0]
    n_heads = sb_bias.shape[1]
    d_attn = w_branch_attn.shape[1]
    d_conv = w_dw.shape[2]
    head_dim = d_attn // n_heads
    scale = head_dim ** -0.5
    page = cache_k.shape[2]

    w_in_b = w_in[0].astype(BF16)
    w_pa, w_pc, w_o = w_branch_attn[0].astype(BF16), w_branch_conv[0].astype(BF16), w_out[0].astype(BF16)
    w_q, sk = w_query[0].astype(BF16), sub_keys[0].astype(BF16)
    eu, ev = expert_u[0].astype(BF16), expert_v[0].astype(BF16)
    tail_w = (w_pa, w_pc, w_o, norm_ffn_g[0], w_q, sk, eu, ev, norm_final_g)
    conv_w = (w_dw[0], b_dw[0], conv_ln_g[0], conv_ln_b[0])

    xp = x_prompt.reshape(bsz * t, d)
    xs = x_sample.reshape(n_dec, d)
    xe = jnp.concatenate([meta_tokens, xs], axis=0)

    def project(x):
        h = _rmsnorm(x, norm_mix_g[0], BF16)
        qkv = _qkv_proj(h, w_in_b, d_attn, scale)
        conv_in = _glu_proj(h, w_in_b, 3 * d_attn, d_conv)
        gates = _gate_proj(h, w_in_b, 3 * d_attn + 2 * d_conv, 2 * d)
        return qkv, conv_in, gates

    (q_p, k_p, v_p, kb_p, vb_p), cin_p, gates_p = project(xp)
    (q_e, k_e, v_e, kb_e, vb_e), cin_e, gates_e = project(xe)

    shp = lambda a: a.reshape(bsz, t, a.shape[-1])
    attn_p = _attn_prompt(shp(q_p), shp(kb_p), shp(vb_p), kb_e[:n_meta], vb_e[:n_meta], sb_bias[0], head_dim)
    cact_p = _conv_prompt(shp(cin_p), cin_e[:n_meta], *conv_w)
    y_p = _token_mixer_tail(xp, attn_p.reshape(bsz * t, d_attn), cact_p.reshape(bsz * t, d_conv), gates_p, *tail_w)

    pool = lambda c: c[0].reshape(c.shape[1], page, d_attn)
    attn_s = _attn_decode(q_e[n_meta:], pool(cache_k), pool(cache_v), page_table, sb_bias[0], head_dim)
    cact_s = _conv_decode(state_conv[0], cin_e[n_meta:], *conv_w)
    y_s = _token_mixer_tail(xs, attn_s, cact_s, gates_e[n_meta:], *tail_w)

    def with_meta(real, extra):
        meta = jnp.broadcast_to(extra[None, :n_meta], (bsz, n_meta, d_attn))
        full = jnp.concatenate([meta, real.reshape(bsz, t, d_attn)], axis=1)
        return full.reshape(1, bsz, n_meta + t, n_heads, head_dim)

    hist = state_conv.shape[2]
    padded_p = jnp.concatenate([jnp.zeros((bsz, hist, d_conv), F32),
                                jnp.broadcast_to(cin_e[None, :n_meta], (bsz, n_meta, d_conv)),
                                shp(cin_p)[:, -hist:]], axis=1)
    conv_prompt = padded_p[None, :, -hist:]
    conv_sample = jnp.concatenate([state_conv[0], cin_e[n_meta:, None, :]], axis=1)[None, :, -hist:]
    return (y_p.reshape(bsz, t, d), y_s.reshape(n_dec, 1, d),
            with_meta(k_p, k_e), with_meta(v_p, v_e), conv_prompt,
            k_e[n_meta:].reshape(1, n_dec, 1, n_heads, head_dim),
            v_e[n_meta:].reshape(1, n_dec, 1, n_heads, head_dim), conv_sample)
```

```python
import functools

import jax
import jax.numpy as jnp
from jax import lax
from jax.experimental import pallas as pl
from jax.experimental.pallas import tpu as pltpu

F32 = jnp.float32
BF16 = jnp.bfloat16

RMS_EPS = 1e-6
LN_EPS = 1e-5
PEER_TOPK = 16
LANES = 128
QUERY_BLOCK = 128
KEY_BLOCK = 256
CONV_HALO = 32
ROUTE_ROWS = 32
VMEM_LIMIT = 56 * 1024 * 1024


def _params(*sem):
    return pltpu.CompilerParams(dimension_semantics=sem, vmem_limit_bytes=VMEM_LIMIT)


def _tile(n, want):
    if n <= want:
        return n
    t = want
    while n % t:
        t //= 2
    return t


_NT = (((1,), (1,)), ((), ()))
_TN = (((0,), (0,)), ((), ()))


def _rmsnorm_kernel(x_ref, g_ref, o_ref):
    x = x_ref[...]
    ms = jnp.mean(x * x, axis=-1, keepdims=True)
    o_ref[...] = (x * lax.rsqrt(ms + RMS_EPS) * g_ref[...]).astype(o_ref.dtype)


def _rmsnorm(x, g, out_dtype):
    m, d = x.shape
    tm = _tile(m, 512)
    return pl.pallas_call(
        _rmsnorm_kernel,
        grid=(m // tm,),
        in_specs=[pl.BlockSpec((tm, d), lambda i: (i, 0)), pl.BlockSpec((1, d), lambda i: (0, 0))],
        out_specs=pl.BlockSpec((tm, d), lambda i: (i, 0)),
        out_shape=jax.ShapeDtypeStruct((m, d), out_dtype),
        compiler_params=_params("parallel"),
    )(x, g.reshape(1, d))


def _qkv_kernel(h_ref, w_ref, q_ref, k_ref, v_ref, kb_ref, vb_ref, *, scale):
    j = pl.program_id(1)
    r = jnp.dot(h_ref[...], w_ref[...], preferred_element_type=F32)

    @pl.when(j == 0)
    def _():
        q_ref[...] = (r * scale).astype(BF16)

    @pl.when(j == 1)
    def _():
        k_ref[...] = r
        kb_ref[...] = r.astype(BF16)

    @pl.when(j == 2)
    def _():
        v_ref[...] = r
        vb_ref[...] = r.astype(BF16)


def _qkv_proj(h, w_in, d_attn, scale):
    m, d = h.shape
    tm = _tile(m, 512)
    row = lambda i, j: (i, 0)
    out_spec = pl.BlockSpec((tm, d_attn), row)
    return pl.pallas_call(
        functools.partial(_qkv_kernel, scale=scale),
        grid=(m // tm, 3),
        in_specs=[pl.BlockSpec((tm, d), row), pl.BlockSpec((d, d_attn), lambda i, j: (0, j))],
        out_specs=[out_spec] * 5,
        out_shape=[jax.ShapeDtypeStruct((m, d_attn), t) for t in (BF16, F32, F32, BF16, BF16)],
        compiler_params=_params("parallel", "arbitrary"),
    )(h, w_in)


def _glu_kernel(h_ref, wa_ref, wb_ref, o_ref):
    h = h_ref[...]
    a = jnp.dot(h, wa_ref[...], preferred_element_type=F32)
    b = jnp.dot(h, wb_ref[...], preferred_element_type=F32)
    o_ref[...] = a * jax.nn.sigmoid(b)


def _glu_proj(h, w_in, col0, d_conv):
    m, d = h.shape
    tm = _tile(m, 512)
    tn = _tile(d_conv, 512)
    a0, b0 = col0 // tn, (col0 + d_conv) // tn
    return pl.pallas_call(
        _glu_kernel,
        grid=(m // tm, d_conv // tn),
        in_specs=[pl.BlockSpec((tm, d), lambda i, j: (i, 0)),
                  pl.BlockSpec((d, tn), lambda i, j: (0, a0 + j)),
                  pl.BlockSpec((d, tn), lambda i, j: (0, b0 + j))],
        out_specs=pl.BlockSpec((tm, tn), lambda i, j: (i, j)),
        out_shape=jax.ShapeDtypeStruct((m, d_conv), F32),
        compiler_params=_params("parallel", "arbitrary"),
    )(h, w_in, w_in)


def _gate_kernel(h_ref, w_ref, o_ref):
    o_ref[...] = jax.nn.sigmoid(jnp.dot(h_ref[...], w_ref[...], preferred_element_type=F32))


def _gate_proj(h, w_in, col0, n_cols):
    m, d = h.shape
    tm = _tile(m, 512)
    tn = _tile(n_cols, 1024)
    c0 = col0 // tn
    return pl.pallas_call(
        _gate_kernel,
        grid=(m // tm, n_cols // tn),
        in_specs=[pl.BlockSpec((tm, d), lambda i, j: (i, 0)),
                  pl.BlockSpec((d, tn), lambda i, j: (0, c0 + j))],
        out_specs=pl.BlockSpec((tm, tn), lambda i, j: (i, j)),
        out_shape=jax.ShapeDtypeStruct((m, n_cols), F32),
        compiler_params=_params("parallel", "arbitrary"),
    )(h, w_in)


LOG2_E = 1.4426950408889634


def _softplus2(z):
    neg_abs = lax.bitcast_convert_type(lax.bitcast_convert_type(z, jnp.uint32) | jnp.uint32(0x80000000), F32)
    return jnp.maximum(z, 0.0) + jnp.log2(1.0 + jnp.exp2(neg_abs))


def _split_bf16(x):
    hi = x.astype(BF16)
    return hi, (x - hi.astype(F32)).astype(BF16)


def _neg_strict_upper(n, copies=1):
    rr = lax.broadcasted_iota(jnp.int32, (copies * n, n), 0) % n
    cc = lax.broadcasted_iota(jnp.int32, (copies * n, n), 1)
    return jnp.where(rr > cc, -1.0, 0.0).astype(BF16)


def _attn_prompt_kernel(bias_ref, q_ref, k_ref, v_ref, km_ref, vm_ref, u_ref, um_ref, o_ref,
                        q2_ref, acc_ref, carry_ref, *, tq, tk, n_meta, head_dim):
    i = pl.program_id(1)
    n_pairs = q_ref.shape[2] // LANES
    lane = lax.broadcasted_iota(jnp.int32, (1, LANES), 1)
    first = lane < head_dim

    def stack_heads(x):
        zero = jnp.zeros_like(x)
        return jnp.concatenate([jnp.where(first, x, zero), jnp.where(first, zero, x)], axis=0)

    for hp in range(n_pairs):
        q2_ref[hp] = stack_heads(q_ref[0, :, hp * LANES:(hp + 1) * LANES])
    acc_ref[...] = jnp.zeros_like(acc_ref)
    carry_ref[...] = jnp.zeros_like(carry_ref)

    def sweep(k_at, v_at, u, mask):
        width = u.shape[1]
        if mask is not None:
            mask = jnp.concatenate([mask, mask], axis=0)
        zs = [lax.dot_general(q2_ref[hp], k_at(hp), _NT, preferred_element_type=F32) for hp in range(n_pairs)]
        parts = []
        for hp in range(n_pairs):
            z = jnp.concatenate([zs[hp][:tq] + bias_ref[2 * hp], zs[hp][tq:] + bias_ref[2 * hp + 1]], axis=0)
            sp = _softplus2(z)
            log_sig = z - sp
            if mask is not None:
                sp = jnp.where(mask, sp, 0.0)
            parts.append((log_sig, jnp.concatenate(_split_bf16(sp), axis=1), jnp.sum(sp, axis=1, keepdims=True)))
        tails = [jnp.dot(hi_lo, u, preferred_element_type=F32) for _, hi_lo, _ in parts]
        weights = []
        for hp in range(n_pairs):
            log_sig, _, row_sum = parts[hp]
            carry = carry_ref[hp]
            a = jnp.exp2(log_sig + tails[hp] + jnp.concatenate([carry] * (width // LANES), axis=1))
            if mask is not None:
                a = jnp.where(mask, a, 0.0)
            carry_ref[hp] = carry - row_sum
            weights.append(jnp.concatenate([a[:tq], a[tq:]], axis=1).astype(BF16))
        for hp in range(n_pairs):
            cols = slice(hp * LANES, (hp + 1) * LANES)
            acc_ref[:, cols] += jnp.dot(weights[hp], stack_heads(v_at(hp)), preferred_element_type=F32)

    def real_block(kb):
        off = pl.multiple_of(kb * tk, tk)
        return (lambda hp: k_ref[0, pl.ds(off, tk), hp * LANES:(hp + 1) * LANES],
                lambda hp: v_ref[0, pl.ds(off, tk), hp * LANES:(hp + 1) * LANES])

    diag = (i * tq) // tk
    q_idx = i * tq + lax.broadcasted_iota(jnp.int32, (tq, tk), 0)
    k_idx = diag * tk + lax.broadcasted_iota(jnp.int32, (tq, tk), 1)
    sweep(*real_block(diag), u_ref[...], k_idx < q_idx)

    def body(n, c):
        sweep(*real_block(diag - 1 - n), u_ref[...], None)
        return c

    lax.fori_loop(0, diag, body, 0)
    meta_keys = lax.broadcasted_iota(jnp.int32, (tq, LANES), 1) < n_meta
    sweep(lambda hp: km_ref[:, hp * LANES:(hp + 1) * LANES], lambda hp: vm_ref[:, hp * LANES:(hp + 1) * LANES],
          um_ref[...], meta_keys)
    o_ref[0] = acc_ref[...].astype(o_ref.dtype)


def _attn_prompt(q, kb, vb, k_meta, v_meta, bias, head_dim):
    bsz, t, da = q.shape
    n_meta = k_meta.shape[0]
    tq = _tile(t, QUERY_BLOCK)
    tk = _tile(t, KEY_BLOCK)
    assert 2 * head_dim == LANES and n_meta <= LANES and tk % tq == 0 and tk % LANES == 0
    pad = ((0, LANES - n_meta), (0, 0))
    n_pairs = da // LANES
    const = lambda b, i: (0, 0)
    return pl.pallas_call(
        functools.partial(_attn_prompt_kernel, tq=tq, tk=tk, n_meta=n_meta, head_dim=head_dim),
        grid=(bsz, t // tq),
        in_specs=[pl.BlockSpec(memory_space=pltpu.SMEM),
                  pl.BlockSpec((1, tq, da), lambda b, i: (b, i, 0)),
                  pl.BlockSpec((1, t, da), lambda b, i: (b, 0, 0)),
                  pl.BlockSpec((1, t, da), lambda b, i: (b, 0, 0)),
                  pl.BlockSpec((LANES, da), const),
                  pl.BlockSpec((LANES, da), const),
                  pl.BlockSpec((2 * tk, tk), const),
                  pl.BlockSpec((2 * LANES, LANES), const)],
        out_specs=pl.BlockSpec((1, tq, da), lambda b, i: (b, i, 0)),
        out_shape=jax.ShapeDtypeStruct((bsz, t, da), BF16),
        scratch_shapes=[pltpu.VMEM((n_pairs, 2 * tq, LANES), BF16),
                        pltpu.VMEM((tq, da), F32),
                        pltpu.VMEM((n_pairs, 2 * tq, LANES), F32)],
        compiler_params=_params("parallel", "arbitrary"),
    )(bias, q, kb, vb, jnp.pad(k_meta, pad), jnp.pad(v_meta, pad),
      _neg_strict_upper(tk, 2), _neg_strict_upper(LANES, 2))


def _attn_decode_kernel(pt_ref, q_ref, bias_ref, u_ref, later_ref, *refs, n_pages, n_heads):
    del pt_ref
    k_refs, v_refs, o_ref = refs[:n_pages], refs[n_pages:2 * n_pages], refs[2 * n_pages]
    da, page = k_refs[0].shape[1], k_refs[0].shape[2]
    hd = da // n_heads
    q = q_ref[0].astype(F32)
    z = jnp.concatenate([jnp.sum((k_refs[p][0] * q).reshape(n_heads, hd, page), axis=1)
                         for p in range(n_pages)], axis=0)
    z = z + bias_ref[...]
    sp = _softplus2(z)
    hi, lo = _split_bf16(sp)
    u = u_ref[...]
    tail = jnp.dot(hi, u, preferred_element_type=F32) + jnp.dot(lo, u, preferred_element_type=F32)
    row_sum = jnp.broadcast_to(jnp.sum(sp, axis=1, keepdims=True), sp.shape)
    rhi, rlo = _split_bf16(row_sum)
    later = later_ref[...]
    carry = jnp.dot(later, rhi, preferred_element_type=F32) + jnp.dot(later, rlo, preferred_element_type=F32)
    a = jnp.exp2((z - sp) + tail + carry)
    acc = jnp.zeros((da, page), F32)
    for p in range(n_pages):
        a_p = a[p * n_heads:(p + 1) * n_heads]
        a_wide = jnp.broadcast_to(a_p[:, None, :], (n_heads, hd, page)).reshape(da, page)
        acc = acc + v_refs[p][0] * a_wide
    ahi, alo = _split_bf16(acc)
    ones = jnp.ones((8, page), BF16)
    o = (lax.dot_general(ones, ahi, _NT, preferred_element_type=F32)
         + lax.dot_general(ones, alo, _NT, preferred_element_type=F32))
    o_ref[0] = o[0:1].astype(o_ref.dtype)


def _attn_decode(q, cache_kt, cache_vt, page_table, bias, head_dim):
    n, da = q.shape
    n_pages = page_table.shape[1]
    page = cache_kt.shape[2]
    n_heads = da // head_dim
    rows = n_pages * n_heads
    assert page == LANES
    q_wide = jnp.broadcast_to(q[:, :, None], (n, da, page))
    bias_col = jnp.broadcast_to(jnp.tile(bias, n_pages)[:, None], (rows, page)).astype(F32)
    rr = lax.broadcasted_iota(jnp.int32, (rows, rows), 0)
    cc = lax.broadcasted_iota(jnp.int32, (rows, rows), 1)
    later = jnp.where((rr % n_heads == cc % n_heads) & (cc // n_heads > rr // n_heads), -1.0, 0.0).astype(BF16)
    const = lambda s, pt: (0, 0)

    def page_spec(p):
        return pl.BlockSpec((1, da, page), lambda s, pt, p=p: (pt[s, p], 0, 0))

    grid_spec = pltpu.PrefetchScalarGridSpec(
        num_scalar_prefetch=1,
        grid=(n,),
        in_specs=[pl.BlockSpec((1, da, page), lambda s, pt: (s, 0, 0)),
                  pl.BlockSpec((rows, page), const),
                  pl.BlockSpec((page, page), const),
                  pl.BlockSpec((rows, rows), const)]
                 + [page_spec(p) for p in range(n_pages)] * 2,
        out_specs=pl.BlockSpec((1, 1, da), lambda s, pt: (s, 0, 0)),
    )
    out = pl.pallas_call(
        functools.partial(_attn_decode_kernel, n_pages=n_pages, n_heads=n_heads),
        grid_spec=grid_spec,
        out_shape=jax.ShapeDtypeStruct((n, 1, da), BF16),
        compiler_params=_params("arbitrary"),
    )(page_table, q_wide, bias_col, _neg_strict_upper(page), later,
      *([cache_kt] * n_pages), *([cache_vt] * n_pages))
    return out.reshape(n, da)


def _ln_swish(y, g, b):
    mu = jnp.mean(y, axis=-1, keepdims=True)
    var = jnp.mean(jnp.square(y - mu), axis=-1, keepdims=True)
    yn = (y - mu) * lax.rsqrt(var + LN_EPS) * g + b
    return yn * jax.nn.sigmoid(yn)


def _conv_prompt_kernel(cur_ref, halo_ref, first_ref, w_ref, b_ref, g_ref, beta_ref, o_ref, win_ref, y_ref,
                        *, width, row_chunk):
    i = pl.program_id(1)
    tt, dc = cur_ref.shape[1], cur_ref.shape[2]
    win_ref[:CONV_HALO] = jnp.where(i == 0, first_ref[...], halo_ref[0])
    win_ref[CONV_HALO:] = cur_ref[0]
    base = CONV_HALO - (width - 1)

    def lane_chunk(c, carry):
        cols = pl.ds(pl.multiple_of(c * LANES, LANES), LANES)
        for r in range(tt // row_chunk):
            acc = jnp.zeros((row_chunk, LANES), F32)
            for w in range(width):
                acc = acc + win_ref[pl.ds(r * row_chunk + base + w, row_chunk), cols] * w_ref[pl.ds(w, 1), cols]
            y_ref[pl.ds(r * row_chunk, row_chunk), cols] = acc
        return carry

    lax.fori_loop(0, dc // LANES, lane_chunk, 0)
    o_ref[0] = _ln_swish(y_ref[...] + b_ref[...], g_ref[...], beta_ref[...]).astype(o_ref.dtype)


def _conv_prompt(conv_in, conv_meta, w_dw, b_dw, ln_g, ln_b):
    bsz, t, dc = conv_in.shape
    width = w_dw.shape[0]
    assert width - 1 <= CONV_HALO and dc % LANES == 0
    tt = _tile(t, 128)
    assert tt % CONV_HALO == 0
    first = jnp.concatenate([jnp.zeros((CONV_HALO, dc), F32), conv_meta], axis=0)[-CONV_HALO:]
    per = tt // CONV_HALO
    vec = lambda a: a.reshape(1, dc)
    const = lambda b, i: (0, 0)
    return pl.pallas_call(
        functools.partial(_conv_prompt_kernel, width=width, row_chunk=_tile(tt, 64)),
        grid=(bsz, t // tt),
        in_specs=[pl.BlockSpec((1, tt, dc), lambda b, i: (b, i, 0)),
                  pl.BlockSpec((1, CONV_HALO, dc), lambda b, i: (b, jnp.maximum(i * per - 1, 0), 0)),
                  pl.BlockSpec((CONV_HALO, dc), const),
                  pl.BlockSpec((width, dc), const),
                  pl.BlockSpec((1, dc), const), pl.BlockSpec((1, dc), const), pl.BlockSpec((1, dc), const)],
        out_specs=pl.BlockSpec((1, tt, dc), lambda b, i: (b, i, 0)),
        out_shape=jax.ShapeDtypeStruct((bsz, t, dc), BF16),
        scratch_shapes=[pltpu.VMEM((CONV_HALO + tt, dc), F32), pltpu.VMEM((tt, dc), F32)],
        compiler_params=_params("parallel", "arbitrary"),
    )(conv_in, conv_in, first, w_dw, vec(b_dw), vec(ln_g), vec(ln_b))


def _conv_decode_kernel(state_ref, new_ref, w_ref, b_ref, g_ref, beta_ref, o_ref, *, width):
    y = jnp.sum(state_ref[...] * w_ref[pl.ds(0, width - 1), :][None], axis=1)
    y = y + new_ref[...] * w_ref[pl.ds(width - 1, 1), :] + b_ref[...]
    o_ref[...] = _ln_swish(y, g_ref[...], beta_ref[...]).astype(o_ref.dtype)


def _conv_decode(state, new, w_dw, b_dw, ln_g, ln_b):
    n, hist, dc = state.shape
    width = w_dw.shape[0]
    assert hist == width - 1
    tn = _tile(n, 8)
    vec = lambda a: a.reshape(1, dc)
    const = lambda i: (0, 0)
    return pl.pallas_call(
        functools.partial(_conv_decode_kernel, width=width),
        grid=(n // tn,),
        in_specs=[pl.BlockSpec((tn, hist, dc), lambda i: (i, 0, 0)),
                  pl.BlockSpec((tn, dc), lambda i: (i, 0)),
                  pl.BlockSpec((width, dc), const),
                  pl.BlockSpec((1, dc), const), pl.BlockSpec((1, dc), const), pl.BlockSpec((1, dc), const)],
        out_specs=pl.BlockSpec((tn, dc), lambda i: (i, 0)),
        out_shape=jax.ShapeDtypeStruct((n, dc), BF16),
        compiler_params=_params("parallel"),
    )(state, new, w_dw, vec(b_dw), vec(ln_g), vec(ln_b))


def _merge_kernel(attn_ref, cact_ref, ga_ref, gc_ref, x_ref, wpa_ref, wpc_ref, wo_ref, g_ref, x1_ref, h2_ref):
    br_a = jnp.dot(attn_ref[...], wpa_ref[...], preferred_element_type=F32)
    br_c = jnp.dot(cact_ref[...], wpc_ref[...], preferred_element_type=F32)
    mixed = (ga_ref[...] * br_a + gc_ref[...] * br_c).astype(BF16)
    x1 = x_ref[...] + jnp.dot(mixed, wo_ref[...], preferred_element_type=F32)
    x1_ref[...] = x1
    ms = jnp.mean(x1 * x1, axis=-1, keepdims=True)
    h2_ref[...] = (x1 * lax.rsqrt(ms + RMS_EPS) * g_ref[...]).astype(BF16)


def _merge(attn, cact, gates, x, w_pa, w_pc, w_o, g_ffn):
    m, d = x.shape
    da, dc = attn.shape[1], cact.shape[1]
    tm = _tile(m, 256)
    row = lambda i: (i, 0)
    const = lambda i: (0, 0)
    once = pl.Buffered(1)
    return pl.pallas_call(
        _merge_kernel,
        grid=(m // tm,),
        in_specs=[pl.BlockSpec((tm, da), row), pl.BlockSpec((tm, dc), row),
                  pl.BlockSpec((tm, d), row), pl.BlockSpec((tm, d), lambda i: (i, 1)),
                  pl.BlockSpec((tm, d), row),
                  pl.BlockSpec((da, d), const, pipeline_mode=once),
                  pl.BlockSpec((dc, d), const, pipeline_mode=once),
                  pl.BlockSpec((d, d), const, pipeline_mode=once),
                  pl.BlockSpec((1, d), const)],
        out_specs=[pl.BlockSpec((tm, d), row), pl.BlockSpec((tm, d), row)],
        out_shape=[jax.ShapeDtypeStruct((m, d), F32), jax.ShapeDtypeStruct((m, d), BF16)],
        compiler_params=_params("parallel"),
    )(attn, cact, gates, gates, x, w_pa, w_pc, w_o, g_ffn.reshape(1, d))


def _top_desc(arrays, refs, k):
    def body(r, arrays):
        out = []
        for s, ref in zip(arrays, refs):
            m = jnp.max(s, axis=0, keepdims=True)
            ref[pl.ds(r, 1), :] = m
            out.append(jnp.where(s == m, -jnp.inf, s))
        return tuple(out)

    lax.fori_loop(0, k, body, tuple(arrays))


def _route_kernel(h_ref, wq_ref, sk_ref, s1_ref, thr_ref, s2_ref, e1_ref, e2_ref, qry_ref, ta_ref, tb_ref, tc_ref,
                  *, n_heads, half, topk):
    qry_ref[...] = jnp.dot(h_ref[...], wq_ref[...], preferred_element_type=F32).astype(BF16)
    side = 1
    while (side + 1) * (side + 1) <= topk:
        side += 1
    rank = lax.broadcasted_iota(jnp.int32, (topk, 1), 0)
    for h in range(n_heads):
        s = []
        for p in range(2):
            col = (2 * h + p) * half
            s.append(lax.dot_general(sk_ref[p], qry_ref[:, col:col + half], _NT, preferred_element_type=F32))
        _top_desc(s, (ta_ref, tb_ref), topk)
        a = ta_ref[...]
        b = tb_ref[...]
        cand = [a[p:p + 1] + b for p in range(side)]
        cand += [jnp.where(rank >= side, a + b[q:q + 1], -jnp.inf) for q in range(side)]
        _top_desc([jnp.concatenate(cand, axis=0)], (tc_ref,), topk)
        c = tc_ref[...]
        zsum = jnp.sum(jnp.exp(c - c[0:1]), axis=0, keepdims=True)
        e1 = jnp.exp(s[0] - a[0:1]) / zsum
        by_key = pl.ds(h, e1.shape[0], stride=n_heads)
        for lt in range(e1.shape[1] // LANES):
            cols = slice(lt * LANES, (lt + 1) * LANES)
            s1_ref[lt, by_key, :] = s[0][:, cols]
            e1_ref[lt, by_key, :] = e1[:, cols]
            thr_ref[lt, h:h + 1, :] = c[topk - 1:topk, cols]
        s2_ref[h] = s[1]
        e2_ref[h] = jnp.exp(s[1] - b[0:1])


def _route(h2, w_query, sub_keys):
    m, d = h2.shape
    n_keys, half = sub_keys.shape[1], sub_keys.shape[2]
    qd = w_query.shape[1]
    n_heads = qd // (2 * half)
    tm = _tile(m, 256)
    assert tm % LANES == 0
    by_head = pl.BlockSpec((n_heads, n_keys, tm), lambda i: (0, 0, i))
    by_key = pl.BlockSpec((tm // LANES, n_keys * n_heads, LANES), lambda i: (i, 0, 0))
    shape_by_head = jax.ShapeDtypeStruct((n_heads, n_keys, m), F32)
    shape_by_key = jax.ShapeDtypeStruct((m // LANES, n_keys * n_heads, LANES), F32)
    per_head = pl.BlockSpec((tm // LANES, n_heads, LANES), lambda i: (i, 0, 0))
    shape_per_head = jax.ShapeDtypeStruct((m // LANES, n_heads, LANES), F32)
    return pl.pallas_call(
        functools.partial(_route_kernel, n_heads=n_heads, half=half, topk=PEER_TOPK),
        grid=(m // tm,),
        in_specs=[pl.BlockSpec((tm, d), lambda i: (i, 0)),
                  pl.BlockSpec((d, qd), lambda i: (0, 0)),
                  pl.BlockSpec((2, n_keys, half), lambda i: (0, 0, 0))],
        out_specs=[by_key, per_head, by_head, by_key, by_head],
        out_shape=[shape_by_key, shape_per_head, shape_by_head, shape_by_key, shape_by_head],
        scratch_shapes=[pltpu.VMEM((tm, qd), BF16)] + [pltpu.VMEM((PEER_TOPK, tm), F32)] * 3,
        compiler_params=_params("parallel"),
    )(h2, w_query, sub_keys)


def _peer_kernel(h_ref, u_ref, v_ref, s1_ref, thr_ref, s2_ref, e1_ref, e2_ref, x1_ref, g_ref, y_ref,
                 hu_even, hu_odd, w_ref, acc_ref, *, n_keys, n_tiles):
    e = pl.program_id(1)
    te, tc = hu_even.shape
    n_heads = s2_ref.shape[0]
    chunk = 2 * n_keys if te % (2 * n_keys) == 0 else n_keys
    n_chunks = te // chunk
    a_parts = n_chunks if tc % (n_chunks * LANES) == 0 else 1
    t_part = tc // a_parts

    @pl.when(e == 0)
    def _():
        for ref in (acc_ref, hu_even, hu_odd):
            ref[...] = jnp.zeros_like(ref)

    first_key = jnp.clip(e - 1, 0, n_tiles - 1) * (te // n_keys)

    def step(hu_next, hu_cur):
        for c in range(n_chunks):
            if c < a_parts:
                part = slice(c * t_part, (c + 1) * t_part)
                hu_next[:, part] = lax.dot_general(u_ref[...], h_ref[part, :], _NT, preferred_element_type=F32)
            for il in range(c * (chunk // n_keys), (c + 1) * (chunk // n_keys)):
                heads = pl.ds(pl.multiple_of((first_key + il) * n_heads, n_heads), n_heads)
                for lt in range(tc // LANES):
                    lanes = slice(lt * LANES, (lt + 1) * LANES)
                    s1 = s1_ref[lt, heads, :]
                    e1 = e1_ref[lt, heads, :]
                    thr = thr_ref[lt]
                    for r in range(n_keys // ROUTE_ROWS):
                        sub = slice(r * ROUTE_ROWS, (r + 1) * ROUTE_ROWS)
                        rows = slice(il * n_keys + r * ROUTE_ROWS, il * n_keys + (r + 1) * ROUTE_ROWS)
                        g = jnp.zeros((ROUTE_ROWS, LANES), F32)
                        for h in range(n_heads):
                            hit = s1[h:h + 1] + s2_ref[h, sub, lanes] >= thr[h:h + 1]
                            g = g + jnp.where(hit, e1[h:h + 1] * e2_ref[h, sub, lanes], 0.0)
                        w_ref[rows, lanes] = (jax.nn.gelu(hu_cur[rows, lanes]) * g).astype(BF16)
            rows = slice(c * chunk, (c + 1) * chunk)
            acc_ref[...] += lax.dot_general(w_ref[rows, :], v_ref[rows, :], _TN, preferred_element_type=F32)

    odd = lax.rem(e, 2) == 1

    @pl.when(jnp.logical_not(odd))
    def _():
        step(hu_even, hu_odd)

    @pl.when(odd)
    def _():
        step(hu_odd, hu_even)

    @pl.when(e == pl.num_programs(1) - 1)
    def _():
        x2 = x1_ref[...] + acc_ref[...]
        ms = jnp.mean(x2 * x2, axis=-1, keepdims=True)
        y_ref[...] = x2 * lax.rsqrt(ms + RMS_EPS) * g_ref[...]


def _peer(h2, tables, x1, expert_u, expert_v, g_final):
    m, d = h2.shape
    n_exp = expert_u.shape[0]
    n_heads, n_keys = tables[2].shape[0], tables[2].shape[1]
    tc = _tile(m, 512)
    te = _tile(n_exp, 512)
    assert te % n_keys == 0 and tc % LANES == 0 and n_heads % 8 == 0
    n_tiles = n_exp // te
    tok = lambda c, e: (c, 0)
    by_head = pl.BlockSpec((n_heads, n_keys, tc), lambda c, e: (0, 0, c))
    by_key = pl.BlockSpec((tc // LANES, n_keys * n_heads, LANES), lambda c, e: (c, 0, 0))
    return pl.pallas_call(
        functools.partial(_peer_kernel, n_keys=n_keys, n_tiles=n_tiles),
        grid=(m // tc, n_tiles + 1),
        in_specs=[pl.BlockSpec((tc, d), tok),
                  pl.BlockSpec((te, d), lambda c, e: (jnp.minimum(e, n_tiles - 1), 0)),
                  pl.BlockSpec((te, d), lambda c, e: (jnp.clip(e - 1, 0, n_tiles - 1), 0)),
                  by_key, pl.BlockSpec((tc // LANES, n_heads, LANES), lambda c, e: (c, 0, 0)), by_head, by_key, by_head,
                  pl.BlockSpec((tc, d), tok, pipeline_mode=pl.Buffered(1)),
                  pl.BlockSpec((1, d), lambda c, e: (0, 0))],
        out_specs=pl.BlockSpec((tc, d), tok),
        out_shape=jax.ShapeDtypeStruct((m, d), F32),
        scratch_shapes=[pltpu.VMEM((te, tc), F32)] * 2 + [pltpu.VMEM((te, tc), BF16), pltpu.VMEM((tc, d), F32)],
        compiler_params=_params("parallel", "arbitrary"),
    )(h2, expert_u, expert_v, *tables, x1, g_final.reshape(1, d))


def _token_mixer_tail(x, attn, cact, gates, w_pa, w_pc, w_o, g_ffn, w_query, sub_keys, expert_u, expert_v, g_final):
    x1, h2 = _merge(attn, cact, gates, x, w_pa, w_pc, w_o, g_ffn)
    tables = _route(h2, w_query, sub_keys)
    return _peer(h2, tables, x1, expert_u, expert_v, g_final)


def kernel(x_prompt, x_sample, cache_k, cache_v, state_conv, page_table, meta_tokens, norm_mix_g, w_in, sb_bias,
           w_dw, b_dw, conv_ln_g, conv_ln_b, w_branch_attn, w_branch_conv, w_out, norm_ffn_g, w_query, sub_keys,
           expert_u, expert_v, norm_final_g):
    depth = w_in.shape[0]
    assert depth == 1, "the meta-token shortcut below holds for a single layer"
    bsz, t, d = x_prompt.shape
    n_dec = x_sample.shape[0]
    assert x_sample.shape[1] == 1
    n_meta = meta_tokens.shape[0]
    n_heads = sb_bias.shape[1]
    d_attn = w_branch_attn.shape[1]
    d_conv = w_dw.shape[2]
    head_dim = d_attn // n_heads
    scale = head_dim ** -0.5 * LOG2_E
    bias2 = sb_bias[0] * LOG2_E
    page = cache_k.shape[2]

    w_in_b = w_in[0].astype(BF16)
    w_pa, w_pc, w_o = w_branch_attn[0].astype(BF16), w_branch_conv[0].astype(BF16), w_out[0].astype(BF16)
    w_q, sk = w_query[0].astype(BF16), sub_keys[0].astype(BF16)
    eu, ev = expert_u[0].astype(BF16), expert_v[0].astype(BF16)
    tail_w = (w_pa, w_pc, w_o, norm_ffn_g[0], w_q, sk, eu, ev, norm_final_g)
    conv_w = (w_dw[0], b_dw[0], conv_ln_g[0], conv_ln_b[0])

    xp = x_prompt.reshape(bsz * t, d)
    xs = x_sample.reshape(n_dec, d)
    xe = jnp.concatenate([meta_tokens, xs], axis=0)

    def project(x):
        h = _rmsnorm(x, norm_mix_g[0], BF16)
        qkv = _qkv_proj(h, w_in_b, d_attn, scale)
        conv_in = _glu_proj(h, w_in_b, 3 * d_attn, d_conv)
        gates = _gate_proj(h, w_in_b, 3 * d_attn + 2 * d_conv, 2 * d)
        return qkv, conv_in, gates

    (q_p, k_p, v_p, kb_p, vb_p), cin_p, gates_p = project(xp)
    (q_e, k_e, v_e, kb_e, vb_e), cin_e, gates_e = project(xe)

    shp = lambda a: a.reshape(bsz, t, a.shape[-1])
    attn_p = _attn_prompt(shp(q_p), shp(kb_p), shp(vb_p), kb_e[:n_meta], vb_e[:n_meta], bias2, head_dim)
    cact_p = _conv_prompt(shp(cin_p), cin_e[:n_meta], *conv_w)
    y_p = _token_mixer_tail(xp, attn_p.reshape(bsz * t, d_attn), cact_p.reshape(bsz * t, d_conv), gates_p, *tail_w)

    pool_t = lambda c: jnp.transpose(c[0], (0, 2, 3, 1)).reshape(c.shape[1], d_attn, page)
    attn_s = _attn_decode(q_e[n_meta:], pool_t(cache_k), pool_t(cache_v), page_table, bias2, head_dim)
    cact_s = _conv_decode(state_conv[0], cin_e[n_meta:], *conv_w)
    y_s = _token_mixer_tail(xs, attn_s, cact_s, gates_e[n_meta:], *tail_w)

    def with_meta(real, extra):
        meta = jnp.broadcast_to(extra[None, :n_meta], (bsz, n_meta, d_attn))
        full = jnp.concatenate([meta, real.reshape(bsz, t, d_attn)], axis=1)
        return full.reshape(1, bsz, n_meta + t, n_heads, head_dim)

    hist = state_conv.shape[2]
    padded_p = jnp.concatenate([jnp.zeros((bsz, hist, d_conv), F32),
                                jnp.broadcast_to(cin_e[None, :n_meta], (bsz, n_meta, d_conv)),
                                shp(cin_p)[:, -hist:]], axis=1)
    conv_prompt = padded_p[None, :, -hist:]
    conv_sample = jnp.concatenate([state_conv[0], cin_e[n_meta:, None, :]], axis=1)[None, :, -hist:]
    return (y_p.reshape(bsz, t, d), y_s.reshape(n_dec, 1, d),
            with_meta(k_p, k_e), with_meta(v_p, v_e), conv_prompt,
            k_e[n_meta:].reshape(1, n_dec, 1, n_heads, head_dim),
            v_e[n_meta:].reshape(1, n_dec, 1, n_heads, head_dim), conv_sample)
```

```python
import functools

import jax
import jax.numpy as jnp
from jax import lax
from jax.experimental import pallas as pl
from jax.experimental.pallas import tpu as pltpu

F32 = jnp.float32
BF16 = jnp.bfloat16

RMS_EPS = 1e-6
LN_EPS = 1e-5
PEER_TOPK = 16
LANES = 128
QUERY_BLOCK = 128
KEY_BLOCK = 256
CONV_HALO = 32
ROUTE_ROWS = 32
VMEM_LIMIT = 56 * 1024 * 1024


def _params(*sem):
    return pltpu.CompilerParams(dimension_semantics=sem, vmem_limit_bytes=VMEM_LIMIT)


def _tile(n, want):
    if n <= want:
        return n
    t = want
    while n % t:
        t //= 2
    return t


_NT = (((1,), (1,)), ((), ()))
_TN = (((0,), (0,)), ((), ()))


def _rmsnorm_kernel(x_ref, g_ref, o_ref):
    x = x_ref[...]
    ms = jnp.mean(x * x, axis=-1, keepdims=True)
    o_ref[...] = (x * lax.rsqrt(ms + RMS_EPS) * g_ref[...]).astype(o_ref.dtype)


def _rmsnorm(x, g, out_dtype):
    m, d = x.shape
    tm = _tile(m, 512)
    return pl.pallas_call(
        _rmsnorm_kernel,
        grid=(m // tm,),
        in_specs=[pl.BlockSpec((tm, d), lambda i: (i, 0)), pl.BlockSpec((1, d), lambda i: (0, 0))],
        out_specs=pl.BlockSpec((tm, d), lambda i: (i, 0)),
        out_shape=jax.ShapeDtypeStruct((m, d), out_dtype),
        compiler_params=_params("parallel"),
    )(x, g.reshape(1, d))


def _qkv_kernel(h_ref, w_ref, q_ref, k_ref, v_ref, kb_ref, vb_ref, *, scale):
    j = pl.program_id(1)
    r = jnp.dot(h_ref[...], w_ref[...], preferred_element_type=F32)

    @pl.when(j == 0)
    def _():
        q_ref[...] = (r * scale).astype(BF16)

    @pl.when(j == 1)
    def _():
        k_ref[...] = r
        kb_ref[...] = r.astype(BF16)

    @pl.when(j == 2)
    def _():
        v_ref[...] = r
        vb_ref[...] = r.astype(BF16)


def _qkv_proj(h, w_in, d_attn, scale):
    m, d = h.shape
    tm = _tile(m, 512)
    row = lambda i, j: (i, 0)
    out_spec = pl.BlockSpec((tm, d_attn), row)
    return pl.pallas_call(
        functools.partial(_qkv_kernel, scale=scale),
        grid=(m // tm, 3),
        in_specs=[pl.BlockSpec((tm, d), row), pl.BlockSpec((d, d_attn), lambda i, j: (0, j))],
        out_specs=[out_spec] * 5,
        out_shape=[jax.ShapeDtypeStruct((m, d_attn), t) for t in (BF16, F32, F32, BF16, BF16)],
        compiler_params=_params("parallel", "arbitrary"),
    )(h, w_in)


def _glu_kernel(h_ref, wa_ref, wb_ref, o_ref):
    h = h_ref[...]
    a = jnp.dot(h, wa_ref[...], preferred_element_type=F32)
    b = jnp.dot(h, wb_ref[...], preferred_element_type=F32)
    o_ref[...] = a * jax.nn.sigmoid(b)


def _glu_proj(h, w_in, col0, d_conv):
    m, d = h.shape
    tm = _tile(m, 512)
    tn = _tile(d_conv, 512)
    a0, b0 = col0 // tn, (col0 + d_conv) // tn
    return pl.pallas_call(
        _glu_kernel,
        grid=(m // tm, d_conv // tn),
        in_specs=[pl.BlockSpec((tm, d), lambda i, j: (i, 0)),
                  pl.BlockSpec((d, tn), lambda i, j: (0, a0 + j)),
                  pl.BlockSpec((d, tn), lambda i, j: (0, b0 + j))],
        out_specs=pl.BlockSpec((tm, tn), lambda i, j: (i, j)),
        out_shape=jax.ShapeDtypeStruct((m, d_conv), F32),
        compiler_params=_params("parallel", "arbitrary"),
    )(h, w_in, w_in)


def _gate_kernel(h_ref, w_ref, o_ref):
    o_ref[...] = jax.nn.sigmoid(jnp.dot(h_ref[...], w_ref[...], preferred_element_type=F32)).astype(o_ref.dtype)


def _gate_proj(h, w_in, col0, n_cols):
    m, d = h.shape
    tm = _tile(m, 512)
    tn = _tile(n_cols, 1024)
    c0 = col0 // tn
    return pl.pallas_call(
        _gate_kernel,
        grid=(m // tm, n_cols // tn),
        in_specs=[pl.BlockSpec((tm, d), lambda i, j: (i, 0)),
                  pl.BlockSpec((d, tn), lambda i, j: (0, c0 + j))],
        out_specs=pl.BlockSpec((tm, tn), lambda i, j: (i, j)),
        out_shape=jax.ShapeDtypeStruct((m, n_cols), BF16),
        compiler_params=_params("parallel", "arbitrary"),
    )(h, w_in)


LOG2_E = 1.4426950408889634


def _softplus2(z):
    neg_abs = lax.bitcast_convert_type(lax.bitcast_convert_type(z, jnp.uint32) | jnp.uint32(0x80000000), F32)
    return jnp.maximum(z, 0.0) + jnp.log2(1.0 + jnp.exp2(neg_abs))


def _split_bf16(x):
    hi = x.astype(BF16)
    return hi, (x - hi.astype(F32)).astype(BF16)


def _neg_strict_upper(n):
    rr = lax.broadcasted_iota(jnp.int32, (n, n), 0)
    cc = lax.broadcasted_iota(jnp.int32, (n, n), 1)
    return jnp.where(rr > cc, -1.0, 0.0).astype(BF16)


def _attn_prompt_kernel(bias_ref, q_ref, k_ref, v_ref, km_ref, vm_ref, u_ref, um_ref, o_ref,
                        q2_ref, acc_ref, carry_ref, *, tq, tk, n_meta, head_dim):
    i = pl.program_id(1)
    n_pairs = q_ref.shape[2] // LANES
    lane = lax.broadcasted_iota(jnp.int32, (1, LANES), 1)
    first = lane < head_dim

    def stack_heads(x):
        zero = jnp.zeros_like(x)
        return jnp.concatenate([jnp.where(first, x, zero), jnp.where(first, zero, x)], axis=0)

    for hp in range(n_pairs):
        q2_ref[hp] = stack_heads(q_ref[0, :, hp * LANES:(hp + 1) * LANES])
    acc_ref[...] = jnp.zeros_like(acc_ref)
    carry_ref[...] = jnp.zeros_like(carry_ref)

    def sweep(k_at, v_at, u, mask):
        width = u.shape[0]
        if mask is not None:
            mask = jnp.concatenate([mask, mask], axis=0)
        zs = [lax.dot_general(q2_ref[hp], k_at(hp), _NT, preferred_element_type=F32) for hp in range(n_pairs)]
        parts = []
        for hp in range(n_pairs):
            z = jnp.concatenate([zs[hp][:tq] + bias_ref[2 * hp], zs[hp][tq:] + bias_ref[2 * hp + 1]], axis=0)
            sp = _softplus2(z)
            log_sig = z - sp
            if mask is not None:
                sp = jnp.where(mask, sp, 0.0)
            parts.append((log_sig, sp.astype(BF16), jnp.sum(sp, axis=1, keepdims=True)))
        tails = [jnp.dot(sp_b, u, preferred_element_type=F32) for _, sp_b, _ in parts]
        weights = []
        for hp in range(n_pairs):
            log_sig, _, row_sum = parts[hp]
            carry = carry_ref[hp]
            a = jnp.exp2(log_sig + tails[hp] + jnp.concatenate([carry] * (width // LANES), axis=1))
            if mask is not None:
                a = jnp.where(mask, a, 0.0)
            carry_ref[hp] = carry - row_sum
            weights.append(jnp.concatenate([a[:tq], a[tq:]], axis=1).astype(BF16))
        for hp in range(n_pairs):
            cols = slice(hp * LANES, (hp + 1) * LANES)
            acc_ref[:, cols] += jnp.dot(weights[hp], stack_heads(v_at(hp)), preferred_element_type=F32)

    def real_block(kb):
        off = pl.multiple_of(kb * tk, tk)
        return (lambda hp: k_ref[0, pl.ds(off, tk), hp * LANES:(hp + 1) * LANES],
                lambda hp: v_ref[0, pl.ds(off, tk), hp * LANES:(hp + 1) * LANES])

    diag = (i * tq) // tk
    q_idx = i * tq + lax.broadcasted_iota(jnp.int32, (tq, tk), 0)
    k_idx = diag * tk + lax.broadcasted_iota(jnp.int32, (tq, tk), 1)
    sweep(*real_block(diag), u_ref[...], k_idx < q_idx)

    def body(n, c):
        sweep(*real_block(diag - 1 - n), u_ref[...], None)
        return c

    lax.fori_loop(0, diag, body, 0)
    meta_keys = lax.broadcasted_iota(jnp.int32, (tq, LANES), 1) < n_meta
    sweep(lambda hp: km_ref[:, hp * LANES:(hp + 1) * LANES], lambda hp: vm_ref[:, hp * LANES:(hp + 1) * LANES],
          um_ref[...], meta_keys)
    o_ref[0] = acc_ref[...].astype(o_ref.dtype)


def _attn_prompt(q, kb, vb, k_meta, v_meta, bias, head_dim):
    bsz, t, da = q.shape
    n_meta = k_meta.shape[0]
    tq = _tile(t, QUERY_BLOCK)
    tk = _tile(t, KEY_BLOCK)
    assert 2 * head_dim == LANES and n_meta <= LANES and tk % tq == 0 and tk % LANES == 0
    pad = ((0, LANES - n_meta), (0, 0))
    n_pairs = da // LANES
    const = lambda b, i: (0, 0)
    return pl.pallas_call(
        functools.partial(_attn_prompt_kernel, tq=tq, tk=tk, n_meta=n_meta, head_dim=head_dim),
        grid=(bsz, t // tq),
        in_specs=[pl.BlockSpec(memory_space=pltpu.SMEM),
                  pl.BlockSpec((1, tq, da), lambda b, i: (b, i, 0)),
                  pl.BlockSpec((1, t, da), lambda b, i: (b, 0, 0)),
                  pl.BlockSpec((1, t, da), lambda b, i: (b, 0, 0)),
                  pl.BlockSpec((LANES, da), const),
                  pl.BlockSpec((LANES, da), const),
                  pl.BlockSpec((tk, tk), const),
                  pl.BlockSpec((LANES, LANES), const)],
        out_specs=pl.BlockSpec((1, tq, da), lambda b, i: (b, i, 0)),
        out_shape=jax.ShapeDtypeStruct((bsz, t, da), BF16),
        scratch_shapes=[pltpu.VMEM((n_pairs, 2 * tq, LANES), BF16),
                        pltpu.VMEM((tq, da), F32),
                        pltpu.VMEM((n_pairs, 2 * tq, LANES), F32)],
        compiler_params=_params("parallel", "arbitrary"),
    )(bias, q, kb, vb, jnp.pad(k_meta, pad), jnp.pad(v_meta, pad),
      _neg_strict_upper(tk), _neg_strict_upper(LANES))


def _attn_decode_kernel(pt_ref, q_ref, bias_ref, u_ref, later_ref, *refs, n_pages, n_heads):
    del pt_ref
    k_refs, v_refs, o_ref = refs[:n_pages], refs[n_pages:2 * n_pages], refs[2 * n_pages]
    da, page = k_refs[0].shape[1], k_refs[0].shape[2]
    hd = da // n_heads
    q = q_ref[0].astype(F32)
    z = jnp.concatenate([jnp.sum((k_refs[p][0] * q).reshape(n_heads, hd, page), axis=1)
                         for p in range(n_pages)], axis=0)
    z = z + bias_ref[...]
    sp = _softplus2(z)
    hi, lo = _split_bf16(sp)
    u = u_ref[...]
    tail = jnp.dot(hi, u, preferred_element_type=F32) + jnp.dot(lo, u, preferred_element_type=F32)
    row_sum = jnp.broadcast_to(jnp.sum(sp, axis=1, keepdims=True), sp.shape)
    rhi, rlo = _split_bf16(row_sum)
    later = later_ref[...]
    carry = jnp.dot(later, rhi, preferred_element_type=F32) + jnp.dot(later, rlo, preferred_element_type=F32)
    a = jnp.exp2((z - sp) + tail + carry)
    acc = jnp.zeros((da, page), F32)
    for p in range(n_pages):
        a_p = a[p * n_heads:(p + 1) * n_heads]
        a_wide = jnp.broadcast_to(a_p[:, None, :], (n_heads, hd, page)).reshape(da, page)
        acc = acc + v_refs[p][0] * a_wide
    ahi, alo = _split_bf16(acc)
    ones = jnp.ones((8, page), BF16)
    o = (lax.dot_general(ones, ahi, _NT, preferred_element_type=F32)
         + lax.dot_general(ones, alo, _NT, preferred_element_type=F32))
    o_ref[0] = o[0:1].astype(o_ref.dtype)


def _attn_decode(q, cache_kt, cache_vt, page_table, bias, head_dim):
    n, da = q.shape
    n_pages = page_table.shape[1]
    page = cache_kt.shape[2]
    n_heads = da // head_dim
    rows = n_pages * n_heads
    assert page == LANES
    q_wide = jnp.broadcast_to(q[:, :, None], (n, da, page))
    bias_col = jnp.broadcast_to(jnp.tile(bias, n_pages)[:, None], (rows, page)).astype(F32)
    rr = lax.broadcasted_iota(jnp.int32, (rows, rows), 0)
    cc = lax.broadcasted_iota(jnp.int32, (rows, rows), 1)
    later = jnp.where((rr % n_heads == cc % n_heads) & (cc // n_heads > rr // n_heads), -1.0, 0.0).astype(BF16)
    const = lambda s, pt: (0, 0)

    def page_spec(p):
        return pl.BlockSpec((1, da, page), lambda s, pt, p=p: (pt[s, p], 0, 0))

    grid_spec = pltpu.PrefetchScalarGridSpec(
        num_scalar_prefetch=1,
        grid=(n,),
        in_specs=[pl.BlockSpec((1, da, page), lambda s, pt: (s, 0, 0)),
                  pl.BlockSpec((rows, page), const),
                  pl.BlockSpec((page, page), const),
                  pl.BlockSpec((rows, rows), const)]
                 + [page_spec(p) for p in range(n_pages)] * 2,
        out_specs=pl.BlockSpec((1, 1, da), lambda s, pt: (s, 0, 0)),
    )
    out = pl.pallas_call(
        functools.partial(_attn_decode_kernel, n_pages=n_pages, n_heads=n_heads),
        grid_spec=grid_spec,
        out_shape=jax.ShapeDtypeStruct((n, 1, da), BF16),
        compiler_params=_params("arbitrary"),
    )(page_table, q_wide, bias_col, _neg_strict_upper(page), later,
      *([cache_kt] * n_pages), *([cache_vt] * n_pages))
    return out.reshape(n, da)


def _ln_swish(y, g, b):
    mu = jnp.mean(y, axis=-1, keepdims=True)
    var = jnp.mean(jnp.square(y - mu), axis=-1, keepdims=True)
    yn = (y - mu) * lax.rsqrt(var + LN_EPS) * g + b
    return yn * jax.nn.sigmoid(yn)


def _conv_prompt_kernel(cur_ref, halo_ref, first_ref, w_ref, b_ref, g_ref, beta_ref, o_ref, win_ref, y_ref,
                        *, width, row_chunk):
    i = pl.program_id(1)
    tt, dc = cur_ref.shape[1], cur_ref.shape[2]
    win_ref[:CONV_HALO] = jnp.where(i == 0, first_ref[...], halo_ref[0])
    win_ref[CONV_HALO:] = cur_ref[0]
    base = CONV_HALO - (width - 1)

    def lane_chunk(c, carry):
        cols = pl.ds(pl.multiple_of(c * LANES, LANES), LANES)
        for r in range(tt // row_chunk):
            acc = jnp.zeros((row_chunk, LANES), F32)
            for w in range(width):
                acc = acc + win_ref[pl.ds(r * row_chunk + base + w, row_chunk), cols] * w_ref[pl.ds(w, 1), cols]
            y_ref[pl.ds(r * row_chunk, row_chunk), cols] = acc
        return carry

    lax.fori_loop(0, dc // LANES, lane_chunk, 0)
    o_ref[0] = _ln_swish(y_ref[...] + b_ref[...], g_ref[...], beta_ref[...]).astype(o_ref.dtype)


def _conv_prompt(conv_in, conv_meta, w_dw, b_dw, ln_g, ln_b):
    bsz, t, dc = conv_in.shape
    width = w_dw.shape[0]
    assert width - 1 <= CONV_HALO and dc % LANES == 0
    tt = _tile(t, 128)
    assert tt % CONV_HALO == 0
    first = jnp.concatenate([jnp.zeros((CONV_HALO, dc), F32), conv_meta], axis=0)[-CONV_HALO:]
    per = tt // CONV_HALO
    vec = lambda a: a.reshape(1, dc)
    const = lambda b, i: (0, 0)
    return pl.pallas_call(
        functools.partial(_conv_prompt_kernel, width=width, row_chunk=_tile(tt, 64)),
        grid=(bsz, t // tt),
        in_specs=[pl.BlockSpec((1, tt, dc), lambda b, i: (b, i, 0)),
                  pl.BlockSpec((1, CONV_HALO, dc), lambda b, i: (b, jnp.maximum(i * per - 1, 0), 0)),
                  pl.BlockSpec((CONV_HALO, dc), const),
                  pl.BlockSpec((width, dc), const),
                  pl.BlockSpec((1, dc), const), pl.BlockSpec((1, dc), const), pl.BlockSpec((1, dc), const)],
        out_specs=pl.BlockSpec((1, tt, dc), lambda b, i: (b, i, 0)),
        out_shape=jax.ShapeDtypeStruct((bsz, t, dc), BF16),
        scratch_shapes=[pltpu.VMEM((CONV_HALO + tt, dc), F32), pltpu.VMEM((tt, dc), F32)],
        compiler_params=_params("parallel", "arbitrary"),
    )(conv_in, conv_in, first, w_dw, vec(b_dw), vec(ln_g), vec(ln_b))


def _conv_decode_kernel(state_ref, new_ref, w_ref, b_ref, g_ref, beta_ref, o_ref, *, width):
    y = jnp.sum(state_ref[...] * w_ref[pl.ds(0, width - 1), :][None], axis=1)
    y = y + new_ref[...] * w_ref[pl.ds(width - 1, 1), :] + b_ref[...]
    o_ref[...] = _ln_swish(y, g_ref[...], beta_ref[...]).astype(o_ref.dtype)


def _conv_decode(state, new, w_dw, b_dw, ln_g, ln_b):
    n, hist, dc = state.shape
    width = w_dw.shape[0]
    assert hist == width - 1
    tn = _tile(n, 8)
    vec = lambda a: a.reshape(1, dc)
    const = lambda i: (0, 0)
    return pl.pallas_call(
        functools.partial(_conv_decode_kernel, width=width),
        grid=(n // tn,),
        in_specs=[pl.BlockSpec((tn, hist, dc), lambda i: (i, 0, 0)),
                  pl.BlockSpec((tn, dc), lambda i: (i, 0)),
                  pl.BlockSpec((width, dc), const),
                  pl.BlockSpec((1, dc), const), pl.BlockSpec((1, dc), const), pl.BlockSpec((1, dc), const)],
        out_specs=pl.BlockSpec((tn, dc), lambda i: (i, 0)),
        out_shape=jax.ShapeDtypeStruct((n, dc), BF16),
        compiler_params=_params("parallel"),
    )(state, new, w_dw, vec(b_dw), vec(ln_g), vec(ln_b))


def _merge_kernel(attn_ref, cact_ref, ga_ref, gc_ref, x_ref, wpa_ref, wpc_ref, wo_ref, g_ref, x1_ref, h2_ref):
    br_a = jnp.dot(attn_ref[...], wpa_ref[...], preferred_element_type=F32)
    br_c = jnp.dot(cact_ref[...], wpc_ref[...], preferred_element_type=F32)
    mixed = (ga_ref[...] * br_a + gc_ref[...] * br_c).astype(BF16)
    x1 = x_ref[...] + jnp.dot(mixed, wo_ref[...], preferred_element_type=F32)
    x1_ref[...] = x1
    ms = jnp.mean(x1 * x1, axis=-1, keepdims=True)
    h2_ref[...] = (x1 * lax.rsqrt(ms + RMS_EPS) * g_ref[...]).astype(BF16)


def _merge(attn, cact, gates, x, w_pa, w_pc, w_o, g_ffn):
    m, d = x.shape
    da, dc = attn.shape[1], cact.shape[1]
    tm = _tile(m, 256)
    row = lambda i: (i, 0)
    const = lambda i: (0, 0)
    once = pl.Buffered(1)
    return pl.pallas_call(
        _merge_kernel,
        grid=(m // tm,),
        in_specs=[pl.BlockSpec((tm, da), row), pl.BlockSpec((tm, dc), row),
                  pl.BlockSpec((tm, d), row), pl.BlockSpec((tm, d), lambda i: (i, 1)),
                  pl.BlockSpec((tm, d), row),
                  pl.BlockSpec((da, d), const, pipeline_mode=once),
                  pl.BlockSpec((dc, d), const, pipeline_mode=once),
                  pl.BlockSpec((d, d), const, pipeline_mode=once),
                  pl.BlockSpec((1, d), const)],
        out_specs=[pl.BlockSpec((tm, d), row), pl.BlockSpec((tm, d), row)],
        out_shape=[jax.ShapeDtypeStruct((m, d), F32), jax.ShapeDtypeStruct((m, d), BF16)],
        compiler_params=_params("parallel"),
    )(attn, cact, gates, gates, x, w_pa, w_pc, w_o, g_ffn.reshape(1, d))


def _top_desc(arrays, refs, k):
    def body(r, arrays):
        out = []
        for s, ref in zip(arrays, refs):
            m = jnp.max(s, axis=0, keepdims=True)
            ref[pl.ds(r, 1), :] = m
            out.append(jnp.where(s == m, -jnp.inf, s))
        return tuple(out)

    lax.fori_loop(0, k, body, tuple(arrays))


def _route_kernel(h_ref, wq_ref, sk_ref, t2_ref, s2_ref, e1_ref, e2_ref, qry_ref, ta_ref, tb_ref, tc_ref,
                  *, n_heads, half, topk):
    qry_ref[...] = jnp.dot(h_ref[...], wq_ref[...], preferred_element_type=F32).astype(BF16)
    side = 1
    while (side + 1) * (side + 1) <= topk:
        side += 1
    rank = lax.broadcasted_iota(jnp.int32, (topk, 1), 0)
    for h in range(n_heads):
        s = []
        for p in range(2):
            col = (2 * h + p) * half
            s.append(lax.dot_general(sk_ref[p], qry_ref[:, col:col + half], _NT, preferred_element_type=F32))
        _top_desc(s, (ta_ref, tb_ref), topk)
        a = ta_ref[...]
        b = tb_ref[...]
        cand = [a[p:p + 1] + b for p in range(side)]
        cand += [jnp.where(rank >= side, a + b[q:q + 1], -jnp.inf) for q in range(side)]
        _top_desc([jnp.concatenate(cand, axis=0)], (tc_ref,), topk)
        c = tc_ref[...]
        zsum = jnp.sum(jnp.exp(c - c[0:1]), axis=0, keepdims=True)
        e1 = jnp.exp(s[0] - a[0:1]) / zsum
        thr = c[topk - 1:topk]
        t2 = jnp.full(s[0].shape, jnp.inf, F32)
        for q in range(topk):
            t2 = jnp.where(s[0] + b[q:q + 1] >= thr, b[q:q + 1], t2)
        by_key = pl.ds(h, e1.shape[0], stride=n_heads)
        for lt in range(e1.shape[1] // LANES):
            cols = slice(lt * LANES, (lt + 1) * LANES)
            t2_ref[lt, by_key, :] = t2[:, cols]
            e1_ref[lt, by_key, :] = e1[:, cols]
        s2_ref[h] = s[1]
        e2_ref[h] = jnp.exp(s[1] - b[0:1])


def _route(h2, w_query, sub_keys):
    m, d = h2.shape
    n_keys, half = sub_keys.shape[1], sub_keys.shape[2]
    qd = w_query.shape[1]
    n_heads = qd // (2 * half)
    tm = _tile(m, 256)
    assert tm % LANES == 0
    by_head = pl.BlockSpec((n_heads, n_keys, tm), lambda i: (0, 0, i))
    by_key = pl.BlockSpec((tm // LANES, n_keys * n_heads, LANES), lambda i: (i, 0, 0))
    shape_by_head = jax.ShapeDtypeStruct((n_heads, n_keys, m), F32)
    shape_by_key = jax.ShapeDtypeStruct((m // LANES, n_keys * n_heads, LANES), F32)
    return pl.pallas_call(
        functools.partial(_route_kernel, n_heads=n_heads, half=half, topk=PEER_TOPK),
        grid=(m // tm,),
        in_specs=[pl.BlockSpec((tm, d), lambda i: (i, 0)),
                  pl.BlockSpec((d, qd), lambda i: (0, 0)),
                  pl.BlockSpec((2, n_keys, half), lambda i: (0, 0, 0))],
        out_specs=[by_key, by_head, by_key, by_head],
        out_shape=[shape_by_key, shape_by_head, shape_by_key, shape_by_head],
        scratch_shapes=[pltpu.VMEM((tm, qd), BF16)] + [pltpu.VMEM((PEER_TOPK, tm), F32)] * 3,
        compiler_params=_params("parallel"),
    )(h2, w_query, sub_keys)


def _peer_kernel(h_ref, u_ref, v_ref, t2_ref, s2_ref, e1_ref, e2_ref, x1_ref, g_ref, y_ref,
                 hu_even, hu_odd, w_ref, acc_ref, *, n_keys, n_tiles):
    e = pl.program_id(1)
    te, tc = hu_even.shape
    n_heads = s2_ref.shape[0]
    chunk = 2 * n_keys if te % (2 * n_keys) == 0 else n_keys
    n_chunks = te // chunk
    a_parts = n_chunks if tc % (n_chunks * LANES) == 0 else 1
    t_part = tc // a_parts

    @pl.when(e == 0)
    def _():
        for ref in (acc_ref, hu_even, hu_odd):
            ref[...] = jnp.zeros_like(ref)

    first_key = jnp.clip(e - 1, 0, n_tiles - 1) * (te // n_keys)

    def step(hu_next, hu_cur):
        for c in range(n_chunks):
            if c < a_parts:
                part = slice(c * t_part, (c + 1) * t_part)
                hu_next[:, part] = lax.dot_general(u_ref[...], h_ref[part, :], _NT, preferred_element_type=F32)
            for il in range(c * (chunk // n_keys), (c + 1) * (chunk // n_keys)):
                heads = pl.ds(pl.multiple_of((first_key + il) * n_heads, n_heads), n_heads)
                for lt in range(tc // LANES):
                    lanes = slice(lt * LANES, (lt + 1) * LANES)
                    t2 = t2_ref[lt, heads, :]
                    e1 = e1_ref[lt, heads, :]
                    for r in range(n_keys // ROUTE_ROWS):
                        sub = slice(r * ROUTE_ROWS, (r + 1) * ROUTE_ROWS)
                        rows = slice(il * n_keys + r * ROUTE_ROWS, il * n_keys + (r + 1) * ROUTE_ROWS)
                        g = jnp.zeros((ROUTE_ROWS, LANES), F32)
                        for h in range(n_heads):
                            hit = s2_ref[h, sub, lanes] >= t2[h:h + 1]
                            g = g + jnp.where(hit, e1[h:h + 1] * e2_ref[h, sub, lanes], 0.0)
                        w_ref[rows, lanes] = (jax.nn.gelu(hu_cur[rows, lanes]) * g).astype(BF16)
            rows = slice(c * chunk, (c + 1) * chunk)
            acc_ref[...] += lax.dot_general(w_ref[rows, :], v_ref[rows, :], _TN, preferred_element_type=F32)

    odd = lax.rem(e, 2) == 1

    @pl.when(jnp.logical_not(odd))
    def _():
        step(hu_even, hu_odd)

    @pl.when(odd)
    def _():
        step(hu_odd, hu_even)

    @pl.when(e == pl.num_programs(1) - 1)
    def _():
        x2 = x1_ref[...] + acc_ref[...]
        ms = jnp.mean(x2 * x2, axis=-1, keepdims=True)
        y_ref[...] = x2 * lax.rsqrt(ms + RMS_EPS) * g_ref[...]


def _peer(h2, tables, x1, expert_u, expert_v, g_final):
    m, d = h2.shape
    n_exp = expert_u.shape[0]
    n_heads, n_keys = tables[1].shape[0], tables[1].shape[1]
    tc = _tile(m, 512)
    te = _tile(n_exp, 512)
    assert te % n_keys == 0 and tc % LANES == 0 and n_heads % 8 == 0 and n_keys % ROUTE_ROWS == 0
    n_tiles = n_exp // te
    tok = lambda c, e: (c, 0)
    by_head = pl.BlockSpec((n_heads, n_keys, tc), lambda c, e: (0, 0, c))
    by_key = pl.BlockSpec((tc // LANES, n_keys * n_heads, LANES), lambda c, e: (c, 0, 0))
    return pl.pallas_call(
        functools.partial(_peer_kernel, n_keys=n_keys, n_tiles=n_tiles),
        grid=(m // tc, n_tiles + 1),
        in_specs=[pl.BlockSpec((tc, d), tok),
                  pl.BlockSpec((te, d), lambda c, e: (jnp.minimum(e, n_tiles - 1), 0)),
                  pl.BlockSpec((te, d), lambda c, e: (jnp.clip(e - 1, 0, n_tiles - 1), 0)),
                  by_key, by_head, by_key, by_head,
                  pl.BlockSpec((tc, d), tok, pipeline_mode=pl.Buffered(1)),
                  pl.BlockSpec((1, d), lambda c, e: (0, 0))],
        out_specs=pl.BlockSpec((tc, d), tok),
        out_shape=jax.ShapeDtypeStruct((m, d), F32),
        scratch_shapes=[pltpu.VMEM((te, tc), F32)] * 2 + [pltpu.VMEM((te, tc), BF16), pltpu.VMEM((tc, d), F32)],
        compiler_params=_params("parallel", "arbitrary"),
    )(h2, expert_u, expert_v, *tables, x1, g_final.reshape(1, d))


def _token_mixer_tail(x, attn, cact, gates, w_pa, w_pc, w_o, g_ffn, w_query, sub_keys, expert_u, expert_v, g_final):
    x1, h2 = _merge(attn, cact, gates, x, w_pa, w_pc, w_o, g_ffn)
    tables = _route(h2, w_query, sub_keys)
    return _peer(h2, tables, x1, expert_u, expert_v, g_final)


def kernel(x_prompt, x_sample, cache_k, cache_v, state_conv, page_table, meta_tokens, norm_mix_g, w_in, sb_bias,
           w_dw, b_dw, conv_ln_g, conv_ln_b, w_branch_attn, w_branch_conv, w_out, norm_ffn_g, w_query, sub_keys,
           expert_u, expert_v, norm_final_g):
    depth = w_in.shape[0]
    assert depth == 1, "the meta-token shortcut below holds for a single layer"
    bsz, t, d = x_prompt.shape
    n_dec = x_sample.shape[0]
    assert x_sample.shape[1] == 1
    n_meta = meta_tokens.shape[0]
    n_heads = sb_bias.shape[1]
    d_attn = w_branch_attn.shape[1]
    d_conv = w_dw.shape[2]
    head_dim = d_attn // n_heads
    scale = head_dim ** -0.5 * LOG2_E
    bias2 = sb_bias[0] * LOG2_E
    page = cache_k.shape[2]

    w_in_b = w_in[0].astype(BF16)
    w_pa, w_pc, w_o = w_branch_attn[0].astype(BF16), w_branch_conv[0].astype(BF16), w_out[0].astype(BF16)
    w_q, sk = w_query[0].astype(BF16), sub_keys[0].astype(BF16)
    eu, ev = expert_u[0].astype(BF16), expert_v[0].astype(BF16)
    tail_w = (w_pa, w_pc, w_o, norm_ffn_g[0], w_q, sk, eu, ev, norm_final_g)
    conv_w = (w_dw[0], b_dw[0], conv_ln_g[0], conv_ln_b[0])

    xp = x_prompt.reshape(bsz * t, d)
    xs = x_sample.reshape(n_dec, d)
    xe = jnp.concatenate([meta_tokens, xs], axis=0)

    def project(x):
        h = _rmsnorm(x, norm_mix_g[0], BF16)
        qkv = _qkv_proj(h, w_in_b, d_attn, scale)
        conv_in = _glu_proj(h, w_in_b, 3 * d_attn, d_conv)
        gates = _gate_proj(h, w_in_b, 3 * d_attn + 2 * d_conv, 2 * d)
        return qkv, conv_in, gates

    (q_p, k_p, v_p, kb_p, vb_p), cin_p, gates_p = project(xp)
    (q_e, k_e, v_e, kb_e, vb_e), cin_e, gates_e = project(xe)

    shp = lambda a: a.reshape(bsz, t, a.shape[-1])
    attn_p = _attn_prompt(shp(q_p), shp(kb_p), shp(vb_p), kb_e[:n_meta], vb_e[:n_meta], bias2, head_dim)
    cact_p = _conv_prompt(shp(cin_p), cin_e[:n_meta], *conv_w)
    y_p = _token_mixer_tail(xp, attn_p.reshape(bsz * t, d_attn), cact_p.reshape(bsz * t, d_conv), gates_p, *tail_w)

    pool_t = lambda c: jnp.transpose(c[0], (0, 2, 3, 1)).reshape(c.shape[1], d_attn, page)
    attn_s = _attn_decode(q_e[n_meta:], pool_t(cache_k), pool_t(cache_v), page_table, bias2, head_dim)
    cact_s = _conv_decode(state_conv[0], cin_e[n_meta:], *conv_w)
    y_s = _token_mixer_tail(xs, attn_s, cact_s, gates_e[n_meta:], *tail_w)

    def with_meta(real, extra):
        meta = jnp.broadcast_to(extra[None, :n_meta], (bsz, n_meta, d_attn))
        full = jnp.concatenate([meta, real.reshape(bsz, t, d_attn)], axis=1)
        return full.reshape(1, bsz, n_meta + t, n_heads, head_dim)

    hist = state_conv.shape[2]
    padded_p = jnp.concatenate([jnp.zeros((bsz, hist, d_conv), F32),
                                jnp.broadcast_to(cin_e[None, :n_meta], (bsz, n_meta, d_conv)),
                                shp(cin_p)[:, -hist:]], axis=1)
    conv_prompt = padded_p[None, :, -hist:]
    conv_sample = jnp.concatenate([state_conv[0], cin_e[n_meta:, None, :]], axis=1)[None, :, -hist:]
    return (y_p.reshape(bsz, t, d), y_s.reshape(n_dec, 1, d),
            with_meta(k_p, k_e), with_meta(v_p, v_e), conv_prompt,
            k_e[n_meta:].reshape(1, n_dec, 1, n_heads, head_dim),
            v_e[n_meta:].reshape(1, n_dec, 1, n_heads, head_dim), conv_sample)
```
